```python
import jax, jax.numpy as jnp
from jax import lax
import numpy as np

D_MODEL = 1024
BATCH = 16
SEQ = 2048
DEPTH = 2

N_META = 16
HEAD_DIM = 64
N_HEADS = (D_MODEL // 2) // HEAD_DIM
N_KV_HEADS = N_HEADS // 4
Q_GROUP = N_HEADS // N_KV_HEADS
ATTN_WIDTH = N_HEADS * HEAD_DIM
KV_WIDTH = N_KV_HEADS * HEAD_DIM
FOURIER_WIDTH = D_MODEL - ATTN_WIDTH
N_FOURIER_GROUPS = 4
FOURIER_GROUP = FOURIER_WIDTH // N_FOURIER_GROUPS
N_BRANCHES = 2
IN_WIDTH = ATTN_WIDTH + 2 * KV_WIDTH + FOURIER_WIDTH + N_BRANCHES * D_MODEL
WINDOW = 128
BLOCK = 128
ROPE_THETA = 10000.0
D_FF = -(-8 * D_MODEL // (3 * 256)) * 256
EPS = 1e-6

kernel_name = "hybrid_fourier_swa_gated_encoder"


def _rmsnorm(x, gain):
    x32 = x.astype(jnp.float32)
    y = x32 * lax.rsqrt(jnp.mean(x32 * x32, axis=-1, keepdims=True) + EPS)
    return (y * gain.astype(jnp.float32)).astype(x.dtype)


def _rope_tables(length):
    inv_freq = ROPE_THETA ** (-jnp.arange(0, HEAD_DIM, 2, dtype=jnp.float32) / HEAD_DIM)
    ang = jnp.arange(length, dtype=jnp.float32)[:, None] * inv_freq[None, :]
    return jnp.cos(ang)[:, None, :], jnp.sin(ang)[:, None, :]


def _rope(x, cos, sin):
    x32 = x.astype(jnp.float32)
    x1, x2 = jnp.split(x32, 2, axis=-1)
    out = jnp.concatenate([x1 * cos - x2 * sin, x2 * cos + x1 * sin], axis=-1)
    return out.astype(x.dtype)


def _sink_attend(q, k, v, mask, sink):
    scale = HEAD_DIM ** -0.5
    s = jnp.einsum('bnqkgd,bnskd->bnkgqs', q, k).astype(jnp.float32) * scale
    s = jnp.where(mask[None, :, None, None], s, -jnp.inf)
    sk = sink.astype(jnp.float32)[None, None, :, :, None, None]
    m = jnp.maximum(jnp.max(s, axis=-1, keepdims=True), sk)
    p = jnp.exp(s - m)
    p = p / (jnp.sum(p, axis=-1, keepdims=True) + jnp.exp(sk - m))
    return jnp.einsum('bnkgqs,bnskd->bnqkgd', p.astype(v.dtype), v)


def _windowed_gqa(q, k, v, sink):
    B, L = q.shape[0], q.shape[1]
    S = L - N_META
    nb = S // BLOCK
    sink = sink.reshape(N_KV_HEADS, Q_GROUP)
    qm, qr = q[:, :N_META], q[:, N_META:]
    km, kr = k[:, :N_META], k[:, N_META:]
    vm, vr = v[:, :N_META], v[:, N_META:]

    qb = qr.reshape(B, nb, BLOCK, N_KV_HEADS, Q_GROUP, HEAD_DIM)
    pad = ((0, 0), (BLOCK, BLOCK), (0, 0), (0, 0))

    def band(t):
        tp = jnp.pad(t, pad).reshape(B, nb + 2, BLOCK, N_KV_HEADS, HEAD_DIM)
        return jnp.concatenate([tp[:, :-2], tp[:, 1:-1], tp[:, 2:]], axis=2)

    def with_meta(meta, win):
        meta_b = jnp.broadcast_to(meta[:, None], (B, nb, N_META, N_KV_HEADS, HEAD_DIM))
        return jnp.concatenate([meta_b, win], axis=2)

    k_all = with_meta(km, band(kr))
    v_all = with_meta(vm, band(vr))
    s_idx = jnp.arange(BLOCK)[:, None]
    t_idx = jnp.arange(3 * BLOCK)[None, :]
    rel = t_idx - BLOCK - s_idx
    key_pos = jnp.arange(nb)[:, None, None] * BLOCK - BLOCK + t_idx[None]
    wmask = (jnp.abs(rel) <= WINDOW)[None] & (key_pos >= 0) & (key_pos < S)
    mask = jnp.concatenate([jnp.ones((nb, BLOCK, N_META), bool), wmask], axis=-1)
    out_r = _sink_attend(qb, k_all, v_all, mask, sink).reshape(B, S, ATTN_WIDTH)

    qmb = qm.reshape(B, 1, N_META, N_KV_HEADS, Q_GROUP, HEAD_DIM)
    km_q = jnp.concatenate([km, kr[:, :BLOCK]], axis=1)[:, None]
    vm_q = jnp.concatenate([vm, vr[:, :BLOCK]], axis=1)[:, None]
    p_idx = jnp.arange(N_META)[:, None]
    j_idx = jnp.arange(BLOCK)[None, :]
    mmask = jnp.concatenate([jnp.ones((N_META, N_META), bool),
                             (N_META + j_idx - p_idx) <= WINDOW], axis=-1)[None]
    out_m = _sink_attend(qmb, km_q, vm_q, mmask, sink).reshape(B, N_META, ATTN_WIDTH)
    return jnp.concatenate([out_m, out_r], axis=1)


def _fourier_mix(f):
    B, L = f.shape[0], f.shape[1]
    fg = f.astype(jnp.float32).reshape(B, L, N_FOURIER_GROUPS, FOURIER_GROUP)
    out = jnp.fft.fft2(fg, axes=(1, 3), norm='ortho').real
    return out.reshape(B, L, FOURIER_WIDTH).astype(f.dtype)


def setup_inputs(seed: int = 0) -> dict:
    key = jax.random.key(seed)
    ks = jax.random.split(key, 16)
    f32 = jnp.float32

    def w(k, shape, fan_in):
        return jax.random.normal(k, shape, f32) * (fan_in ** -0.5)

    def gain(k):
        return 1.0 + 0.02 * jax.random.normal(k, (DEPTH, D_MODEL), f32)

    return {
        "x": jax.random.normal(ks[0], (BATCH, SEQ, D_MODEL), f32),
        "meta_tokens": jax.random.normal(ks[1], (N_META, D_MODEL), f32),
        "w_in": w(ks[2], (DEPTH, D_MODEL, IN_WIDTH), D_MODEL),
        "w_fourier_out": w(ks[3], (DEPTH, FOURIER_WIDTH, D_MODEL), FOURIER_WIDTH),
        "w_attn_out": w(ks[4], (DEPTH, ATTN_WIDTH, D_MODEL), ATTN_WIDTH),
        "w_o": w(ks[5], (DEPTH, D_MODEL, D_MODEL), D_MODEL),
        "sink_logits": jax.random.normal(ks[6], (DEPTH, N_HEADS), f32),
        "norm_mix_pre": gain(ks[7]),
        "norm_mix_post": gain(ks[8]),
        "norm_ffn_pre": gain(ks[9]),
        "norm_ffn_post": gain(ks[10]),
        "w_ffn_gate": w(ks[11], (DEPTH, D_MODEL, D_FF), D_MODEL),
        "w_ffn_up": w(ks[12], (DEPTH, D_MODEL, D_FF), D_MODEL),
        "w_ffn_down": w(ks[13], (DEPTH, D_FF, D_MODEL), D_FF),
    }


def reference(x, meta_tokens, w_in, w_fourier_out, w_attn_out, w_o, sink_logits,
              norm_mix_pre, norm_mix_post, norm_ffn_pre, norm_ffn_post,
              w_ffn_gate, w_ffn_up, w_ffn_down):
    B = x.shape[0]
    meta = jnp.broadcast_to(meta_tokens[None].astype(x.dtype), (B, N_META, D_MODEL))
    h = jnp.concatenate([meta, x], axis=1)
    L = h.shape[1]
    cos, sin = _rope_tables(L)
    splits = np.cumsum([ATTN_WIDTH, KV_WIDTH, KV_WIDTH, FOURIER_WIDTH]).tolist()

    for l in range(DEPTH):
        u = _rmsnorm(h, norm_mix_pre[l])
        proj = u @ w_in[l]
        q, k, v, f, g = jnp.split(proj, splits, axis=-1)
        q = _rope(q.reshape(B, L, N_HEADS, HEAD_DIM), cos, sin)
        k = _rope(k.reshape(B, L, N_KV_HEADS, HEAD_DIM), cos, sin)
        v = v.reshape(B, L, N_KV_HEADS, HEAD_DIM)
        y_attn = _windowed_gqa(q, k, v, sink_logits[l]) @ w_attn_out[l]
        y_four = _fourier_mix(f) @ w_fourier_out[l]
        g_four, g_attn = jnp.split(jax.nn.sigmoid(g), N_BRANCHES, axis=-1)
        mixed = (g_four * y_four + g_attn * y_attn) @ w_o[l]
        h = h + _rmsnorm(mixed, norm_mix_post[l])

        u = _rmsnorm(h, norm_ffn_pre[l])
        ff = (jax.nn.silu(u @ w_ffn_gate[l]) * (u @ w_ffn_up[l])) @ w_ffn_down[l]
        h = h + _rmsnorm(ff, norm_ffn_post[l])

    return h[:, N_META:]
```

```python
import functools

import jax
import jax.numpy as jnp
import numpy as np
from jax import lax
from jax.experimental import pallas as pl
from jax.experimental.pallas import tpu as pltpu

D_MODEL = 1024
N_META = 16
HEAD_DIM = 64
N_HEADS = 8
N_KV_HEADS = 2
Q_GROUP = N_HEADS // N_KV_HEADS
ATTN_WIDTH = N_HEADS * HEAD_DIM
KV_WIDTH = N_KV_HEADS * HEAD_DIM
FOURIER_WIDTH = D_MODEL - ATTN_WIDTH
FOURIER_GROUP = 128
N_FOURIER_GROUPS = FOURIER_WIDTH // FOURIER_GROUP
GATE_WIDTH = 2 * D_MODEL
IN_WIDTH = ATTN_WIDTH + 2 * KV_WIDTH + FOURIER_WIDTH + GATE_WIDTH
WINDOW = 128
BLOCK = 128
ROPE_THETA = 10000.0
D_FF = 2816
EPS = 1e-6

LANES = 128
VMEM_LIMIT = 56 * 1024 * 1024
ROW_TILE = 688
FF_CHUNKS = ((0, 1024), (1024, 2048), (2048, D_FF))

BF16 = jnp.bfloat16
F32 = jnp.float32


def _const_spec(shape):
    return pl.BlockSpec(shape, lambda *_: (0,) * len(shape), pipeline_mode=pl.Buffered(1))


def _params():
    return pltpu.CompilerParams(dimension_semantics=("arbitrary",), vmem_limit_bytes=VMEM_LIMIT)


def _rms(x, gain):
    return x * lax.rsqrt(jnp.mean(x * x, axis=-1, keepdims=True) + EPS) * gain


def _inproj_kernel(h_ref, gain_ref, w_ref, cos_ref, sin_ref, q_ref, k_ref, v_ref, f_ref, g_ref):
    u = _rms(h_ref[...], gain_ref[...]).astype(BF16)
    cos = cos_ref[...]
    sin = sin_ref[...]
    first_half = (lax.broadcasted_iota(jnp.int32, (1, LANES), 1) % HEAD_DIM) < HEAD_DIM // 2

    def rope(y):
        rot = jnp.where(first_half, pltpu.roll(y, LANES - HEAD_DIM // 2, 1),
                        pltpu.roll(y, HEAD_DIM // 2, 1))
        return y * cos + rot * sin

    def proj(lo, hi):
        return jnp.dot(u, w_ref[:, lo:hi], preferred_element_type=F32)

    q = proj(0, ATTN_WIDTH)
    scale = HEAD_DIM ** -0.5
    for c in range(ATTN_WIDTH // LANES):
        q_ref[:, c * LANES:(c + 1) * LANES] = (rope(q[:, c * LANES:(c + 1) * LANES]) * scale).astype(BF16)
    kv = proj(ATTN_WIDTH, ATTN_WIDTH + 2 * KV_WIDTH)
    k_ref[...] = rope(kv[:, :KV_WIDTH]).astype(BF16)
    v_ref[...] = kv[:, KV_WIDTH:].astype(BF16)
    f0 = ATTN_WIDTH + 2 * KV_WIDTH
    f_ref[...] = proj(f0, f0 + FOURIER_WIDTH).astype(BF16)
    g0 = f0 + FOURIER_WIDTH
    for c in range(GATE_WIDTH // 1024):
        g_ref[:, c * 1024:(c + 1) * 1024] = proj(g0 + c * 1024, g0 + (c + 1) * 1024).astype(BF16)


def _inproj(h, gain, w, cos_t, sin_t, seq_len):
    n = h.shape[0]
    tiles_per_seq = seq_len // ROW_TILE
    row = lambda width: pl.BlockSpec((ROW_TILE, width), lambda i: (i, 0))
    table = pl.BlockSpec((ROW_TILE, LANES), lambda i: (i % tiles_per_seq, 0))
    widths = (ATTN_WIDTH, KV_WIDTH, KV_WIDTH, FOURIER_WIDTH, GATE_WIDTH)
    return pl.pallas_call(
        _inproj_kernel,
        out_shape=[jax.ShapeDtypeStruct((n, wd), BF16) for wd in widths],
        grid=(n // ROW_TILE,),
        in_specs=[row(D_MODEL), _const_spec((1, D_MODEL)), _const_spec((D_MODEL, IN_WIDTH)), table, table],
        out_specs=[row(wd) for wd in widths],
        compiler_params=_params(),
        name="inproj",
    )(h, gain, w, cos_t, sin_t)


def _attend(q_rows, k_all, v_all, mask, sink_ref):
    r = q_rows.shape[0]
    low_lanes = lax.broadcasted_iota(jnp.int32, (1, LANES), 1) < HEAD_DIM
    top_rows = lax.broadcasted_iota(jnp.int32, (2 * r, 1), 0) < r
    mask2 = jnp.concatenate([mask, mask], axis=0)
    zero = jnp.zeros((), BF16)
    outs = []
    for j in range(Q_GROUP):
        qj = q_rows[:, j * LANES:(j + 1) * LANES]
        qm = jnp.concatenate([jnp.where(low_lanes, qj, zero), jnp.where(low_lanes, zero, qj)], axis=0)
        s = lax.dot_general(qm, k_all, (((1,), (1,)), ((), ())), preferred_element_type=F32)
        s = jnp.where(mask2, s, -1e30)
        sink = jnp.where(top_rows, sink_ref[j], sink_ref[Q_GROUP + j])
        m = jnp.maximum(jnp.max(s, axis=-1, keepdims=True), sink)
        p = jnp.exp(s - m)
        denom = jnp.sum(p, axis=-1, keepdims=True) + jnp.exp(sink - m)
        o = jnp.dot(p.astype(BF16), v_all, preferred_element_type=F32) / denom
        outs.append(jnp.where(low_lanes, o[:r], o[r:]))
    return jnp.concatenate(outs, axis=1)


def _attn_kernel(sink_ref, q_ref, k_ref, v_ref, o_ref):
    seq = q_ref.shape[0] - N_META
    n_blocks = seq // BLOCK
    band = 3 * BLOCK
    k_head = k_ref[0:BLOCK, :]
    v_head = v_ref[0:BLOCK, :]

    n_keys = 2 * BLOCK
    col = lax.broadcasted_iota(jnp.int32, (N_META, n_keys), 1)
    p_idx = lax.broadcasted_iota(jnp.int32, (N_META, n_keys), 0)
    meta_mask = (col < N_META) | (col - p_idx <= WINDOW)
    o_meta = _attend(q_ref[0:N_META, :], k_ref[0:n_keys, :], v_ref[0:n_keys, :], meta_mask, sink_ref)
    o_ref[0:N_META, :] = o_meta.astype(BF16)

    col = lax.broadcasted_iota(jnp.int32, (BLOCK, BLOCK + band), 1)
    row = lax.broadcasted_iota(jnp.int32, (BLOCK, BLOCK + band), 0)

    def body(i, carry):
        q_start = pl.multiple_of(N_META + BLOCK * i, N_META)
        kb = jnp.clip(i - 1, 0, n_blocks - 3)
        k_start = pl.multiple_of(N_META + BLOCK * kb, N_META)
        rel = (col - BLOCK) + BLOCK * (kb - i) - row
        mask = (col < N_META) | ((col >= BLOCK) & (jnp.abs(rel) <= WINDOW))
        k_all = jnp.concatenate([k_head, k_ref[pl.ds(k_start, band), :]], axis=0)
        v_all = jnp.concatenate([v_head, v_ref[pl.ds(k_start, band), :]], axis=0)
        o = _attend(q_ref[pl.ds(q_start, BLOCK), :], k_all, v_all, mask, sink_ref)
        o_ref[pl.ds(q_start, BLOCK), :] = o.astype(BF16)
        return carry

    lax.fori_loop(0, n_blocks, body, 0)


def _attention(sink, q, k, v, seq_len):
    n = q.shape[0]
    per_seq = lambda width: pl.BlockSpec((seq_len, width), lambda b: (b, 0))
    return pl.pallas_call(
        _attn_kernel,
        out_shape=jax.ShapeDtypeStruct((n, ATTN_WIDTH), BF16),
        grid=(n // seq_len,),
        in_specs=[pl.BlockSpec(memory_space=pltpu.SMEM), per_seq(ATTN_WIDTH), per_seq(KV_WIDTH), per_seq(KV_WIDTH)],
        out_specs=per_seq(ATTN_WIDTH),
        compiler_params=_params(),
        name="attn",
    )(sink, q, k, v)


def _fourier_kernel(f_ref, chan_ref, pos_ref, y_ref, xcs_ref):
    seq_len = f_ref.shape[0]
    for g in range(N_FOURIER_GROUPS):
        lanes = slice(g * FOURIER_GROUP, (g + 1) * FOURIER_GROUP)
        t = jnp.dot(f_ref[:, lanes], chan_ref[...], preferred_element_type=F32)
        xcs_ref[0:seq_len, lanes] = t[:, :FOURIER_GROUP].astype(BF16)
        xcs_ref[seq_len:2 * seq_len, lanes] = t[:, FOURIER_GROUP:].astype(BF16)
    y_ref[...] = jnp.dot(pos_ref[...], xcs_ref[...], preferred_element_type=F32).astype(BF16)


def _fourier(f, chan_t, pos_t, seq_len):
    n = f.shape[0]
    per_seq = pl.BlockSpec((seq_len, FOURIER_WIDTH), lambda b: (b, 0))
    return pl.pallas_call(
        _fourier_kernel,
        out_shape=jax.ShapeDtypeStruct((n, FOURIER_WIDTH), BF16),
        grid=(n // seq_len,),
        in_specs=[per_seq, _const_spec((FOURIER_GROUP, 2 * FOURIER_GROUP)), _const_spec((seq_len, 2 * seq_len))],
        out_specs=per_seq,
        scratch_shapes=[pltpu.VMEM((2 * seq_len, FOURIER_WIDTH), BF16)],
        compiler_params=_params(),
        name="fourier",
    )(f, chan_t, pos_t)


def _mixout_kernel(yf_ref, ya_ref, g_ref, h_ref, wf_ref, wa_ref, wo_ref, gain_ref, o_ref):
    y_four = jnp.dot(yf_ref[...], wf_ref[...], preferred_element_type=F32)
    y_attn = jnp.dot(ya_ref[...], wa_ref[...], preferred_element_type=F32)
    g_four = jax.nn.sigmoid(g_ref[:, :D_MODEL].astype(F32))
    g_attn = jax.nn.sigmoid(g_ref[:, D_MODEL:].astype(F32))
    mixed = (g_four * y_four + g_attn * y_attn).astype(BF16)
    out = jnp.dot(mixed, wo_ref[...], preferred_element_type=F32)
    o_ref[...] = h_ref[...] + _rms(out, gain_ref[...])


def _mixout(yf, ya, g, h, wf, wa, wo, gain):
    n = h.shape[0]
    row = lambda width: pl.BlockSpec((ROW_TILE, width), lambda i: (i, 0))
    return pl.pallas_call(
        _mixout_kernel,
        out_shape=jax.ShapeDtypeStruct((n, D_MODEL), F32),
        grid=(n // ROW_TILE,),
        in_specs=[row(FOURIER_WIDTH), row(ATTN_WIDTH), row(GATE_WIDTH), row(D_MODEL),
                  _const_spec((FOURIER_WIDTH, D_MODEL)), _const_spec((ATTN_WIDTH, D_MODEL)),
                  _const_spec((D_MODEL, D_MODEL)), _const_spec((1, D_MODEL))],
        out_specs=row(D_MODEL),
        compiler_params=_params(),
        name="mixout",
    )(yf, ya, g, h, wf, wa, wo, gain)


def _ffn_kernel(h_ref, gain_pre_ref, wg_ref, wu_ref, wd_ref, gain_post_ref, o_ref, act_ref):
    h = h_ref[...]
    u = _rms(h, gain_pre_ref[...]).astype(BF16)
    for lo, hi in FF_CHUNKS:
        gate = jnp.dot(u, wg_ref[:, lo:hi], preferred_element_type=F32)
        up = jnp.dot(u, wu_ref[:, lo:hi], preferred_element_type=F32)
        act_ref[:, lo:hi] = (gate * jax.nn.sigmoid(gate) * up).astype(BF16)
    ff = jnp.dot(act_ref[...], wd_ref[...], preferred_element_type=F32)
    o_ref[...] = h + _rms(ff, gain_post_ref[...])


def _ffn(h, gain_pre, wg, wu, wd, gain_post):
    n = h.shape[0]
    row = pl.BlockSpec((ROW_TILE, D_MODEL), lambda i: (i, 0))
    return pl.pallas_call(
        _ffn_kernel,
        out_shape=jax.ShapeDtypeStruct((n, D_MODEL), F32),
        grid=(n // ROW_TILE,),
        in_specs=[row, _const_spec((1, D_MODEL)), _const_spec((D_MODEL, D_FF)), _const_spec((D_MODEL, D_FF)),
                  _const_spec((D_FF, D_MODEL)), _const_spec((1, D_MODEL))],
        out_specs=row,
        scratch_shapes=[pltpu.VMEM((ROW_TILE, D_FF), BF16)],
        compiler_params=_params(),
        name="ffn",
    )(h, gain_pre, wg, wu, wd, gain_post)


def _rope_tables(seq_len):
    d = jnp.arange(LANES) % HEAD_DIM
    inv_freq = ROPE_THETA ** (-(2 * (d % (HEAD_DIM // 2))).astype(F32) / HEAD_DIM)
    ang = jnp.arange(seq_len, dtype=F32)[:, None] * inv_freq[None, :]
    sign = jnp.where(d < HEAD_DIM // 2, -1.0, 1.0).astype(F32)
    return jnp.cos(ang), jnp.sin(ang) * sign[None, :]


def _dft_tables(n):
    idx = jnp.arange(n, dtype=jnp.int32)
    ang = ((idx[:, None] * idx[None, :]) % n).astype(F32) * (2.0 * np.pi / n)
    scale = n ** -0.5
    return jnp.cos(ang) * scale, jnp.sin(ang) * scale


def _q_lane_order():
    d = np.arange(HEAD_DIM)
    return np.concatenate([np.concatenate([j * HEAD_DIM + d, (Q_GROUP + j) * HEAD_DIM + d])
                           for j in range(Q_GROUP)])


def kernel(x, meta_tokens, w_in, w_fourier_out, w_attn_out, w_o, sink_logits, norm_mix_pre, norm_mix_post,
           norm_ffn_pre, norm_ffn_post, w_ffn_gate, w_ffn_up, w_ffn_down):
    batch, seq, _ = x.shape
    seq_len = N_META + seq
    depth = w_in.shape[0]
    assert seq_len % ROW_TILE == 0 and seq % BLOCK == 0 and seq // BLOCK >= 3

    meta = jnp.broadcast_to(meta_tokens[None].astype(x.dtype), (batch, N_META, D_MODEL))
    h = jnp.concatenate([meta, x], axis=1).reshape(batch * seq_len, D_MODEL)

    cos_t, sin_t = _rope_tables(seq_len)
    chan_c, chan_s = _dft_tables(FOURIER_GROUP)
    chan_t = jnp.concatenate([chan_c, chan_s], axis=1).astype(BF16)
    pos_c, pos_s = _dft_tables(seq_len)
    pos_t = jnp.concatenate([pos_c, -pos_s], axis=1).astype(BF16)

    order = _q_lane_order()
    w_in_cols = np.concatenate([order, np.arange(ATTN_WIDTH, IN_WIDTH)])
    gain = lambda t, l: t[l].reshape(1, D_MODEL)

    for l in range(depth):
        q, k, v, f, g = _inproj(h, gain(norm_mix_pre, l), w_in[l][:, w_in_cols].astype(BF16), cos_t, sin_t, seq_len)
        ya = _attention(sink_logits[l], q, k, v, seq_len)
        yf = _fourier(f, chan_t, pos_t, seq_len)
        h = _mixout(yf, ya, g, h, w_fourier_out[l].astype(BF16), w_attn_out[l][order].astype(BF16),
                    w_o[l].astype(BF16), gain(norm_mix_post, l))
        h = _ffn(h, gain(norm_ffn_pre, l), w_ffn_gate[l].astype(BF16), w_ffn_up[l].astype(BF16),
                 w_ffn_down[l].astype(BF16), gain(norm_ffn_post, l))

    return h.reshape(batch, seq_len, D_MODEL)[:, N_META:]
```

```python
import functools

import jax
import jax.numpy as jnp
import numpy as np
from jax import lax
from jax.experimental import pallas as pl
from jax.experimental.pallas import tpu as pltpu

D_MODEL = 1024
N_META = 16
HEAD_DIM = 64
N_HEADS = 8
N_KV_HEADS = 2
Q_GROUP = N_HEADS // N_KV_HEADS
ATTN_WIDTH = N_HEADS * HEAD_DIM
KV_WIDTH = N_KV_HEADS * HEAD_DIM
FOURIER_WIDTH = D_MODEL - ATTN_WIDTH
FOURIER_GROUP = 128
N_FOURIER_GROUPS = FOURIER_WIDTH // FOURIER_GROUP
GATE_WIDTH = 2 * D_MODEL
IN_WIDTH = ATTN_WIDTH + 2 * KV_WIDTH + FOURIER_WIDTH + GATE_WIDTH
PROJ_WIDTHS = (ATTN_WIDTH, KV_WIDTH, KV_WIDTH, FOURIER_WIDTH, GATE_WIDTH)
WINDOW = 128
BLOCK = 128
ROPE_THETA = 10000.0
D_FF = 2816
EPS = 1e-6

LANES = 128
VMEM_LIMIT = 56 * 1024 * 1024
ROW_TILE = 1024
FF_CHUNKS = ((0, 1024), (1024, 2048), (2048, D_FF))

BF16 = jnp.bfloat16
F32 = jnp.float32


def _const_spec(shape):
    return pl.BlockSpec(shape, lambda *_: (0,) * len(shape), pipeline_mode=pl.Buffered(1))


def _params():
    return pltpu.CompilerParams(dimension_semantics=("arbitrary",), vmem_limit_bytes=VMEM_LIMIT)


def _rms(x, gain):
    return x * lax.rsqrt(jnp.mean(x * x, axis=-1, keepdims=True) + EPS) * gain


def _token_call(body, name, seq_ins, meta_ins, consts, out_widths, out_dtype, scratch_shapes=()):
    n_seq = seq_ins[0][0].shape[0]
    n_meta = meta_ins[0].shape[0]
    n_tiles = n_seq // ROW_TILE
    n_in, n_const, n_out = len(seq_ins), len(consts), len(out_widths)

    def kernel(*refs):
        seq_refs, refs = refs[:n_in], refs[n_in:]
        meta_refs, refs = refs[:n_in], refs[n_in:]
        const_refs, refs = refs[:n_const], refs[n_const:]
        out_seq, refs = refs[:n_out], refs[n_out:]
        out_meta, scratch = refs[:n_out], refs[n_out:]
        step = pl.program_id(0)

        @pl.when(step < n_tiles)
        def _():
            body(seq_refs, const_refs, out_seq, scratch)

        @pl.when(step == n_tiles)
        def _():
            body(meta_refs, const_refs, out_meta, scratch)

    last = n_tiles - 1
    seq_specs = []
    for arr, index_fn in seq_ins:
        index_fn = index_fn or (lambda i: i)
        seq_specs.append(pl.BlockSpec((ROW_TILE, arr.shape[1]),
                                      functools.partial(lambda fn, i: (fn(jnp.minimum(i, last)), 0), index_fn)))
    meta_specs = [pl.BlockSpec(arr.shape, lambda i: (0, 0)) for arr in meta_ins]
    const_specs = [_const_spec(c.shape) for c in consts]
    out_shape = ([jax.ShapeDtypeStruct((n_seq, w), out_dtype) for w in out_widths]
                 + [jax.ShapeDtypeStruct((n_meta, w), out_dtype) for w in out_widths])
    out_specs = ([pl.BlockSpec((ROW_TILE, w), lambda i: (jnp.minimum(i, last), 0)) for w in out_widths]
                 + [pl.BlockSpec((n_meta, w), lambda i: (0, 0)) for w in out_widths])
    outs = pl.pallas_call(
        kernel,
        out_shape=out_shape,
        grid=(n_tiles + 1,),
        in_specs=seq_specs + meta_specs + const_specs,
        out_specs=out_specs,
        scratch_shapes=list(scratch_shapes),
        compiler_params=_params(),
        name=name,
    )(*[a for a, _ in seq_ins], *meta_ins, *consts)
    return outs[:n_out], outs[n_out:]


def _inproj_body(in_refs, const_refs, out_refs, scratch):
    h_ref, cos_ref, sin_ref = in_refs
    gain_ref, w_ref = const_refs
    q_ref, k_ref, v_ref, f_ref, g_ref = out_refs
    u = _rms(h_ref[...], gain_ref[...]).astype(BF16)
    cos = cos_ref[...]
    sin = sin_ref[...]
    first_half = (lax.broadcasted_iota(jnp.int32, (1, LANES), 1) % HEAD_DIM) < HEAD_DIM // 2

    def rope(y):
        rot = jnp.where(first_half, pltpu.roll(y, LANES - HEAD_DIM // 2, 1),
                        pltpu.roll(y, HEAD_DIM // 2, 1))
        return y * cos + rot * sin

    def proj(lo, hi):
        return jnp.dot(u, w_ref[:, lo:hi], preferred_element_type=F32)

    q = proj(0, ATTN_WIDTH)
    scale = HEAD_DIM ** -0.5
    for c in range(ATTN_WIDTH // LANES):
        q_ref[:, c * LANES:(c + 1) * LANES] = (rope(q[:, c * LANES:(c + 1) * LANES]) * scale).astype(BF16)
    kv = proj(ATTN_WIDTH, ATTN_WIDTH + 2 * KV_WIDTH)
    k_ref[...] = rope(kv[:, :KV_WIDTH]).astype(BF16)
    v_ref[...] = kv[:, KV_WIDTH:].astype(BF16)
    f0 = ATTN_WIDTH + 2 * KV_WIDTH
    f_ref[...] = proj(f0, f0 + FOURIER_WIDTH).astype(BF16)
    g0 = f0 + FOURIER_WIDTH
    for c in range(GATE_WIDTH // 1024):
        g_ref[:, c * 1024:(c + 1) * 1024] = proj(g0 + c * 1024, g0 + (c + 1) * 1024).astype(BF16)


def _inproj(h, gain, w, rope_seq, rope_meta, seq):
    tiles_per_seq = seq // ROW_TILE
    table_index = lambda i: i % tiles_per_seq
    seq_ins = [(h[0], None), (rope_seq[0], table_index), (rope_seq[1], table_index)]
    return _token_call(_inproj_body, "inproj", seq_ins, [h[1], rope_meta[0], rope_meta[1]], [gain, w],
                       PROJ_WIDTHS, BF16)


def _attend(q_rows, k_all, v_all, mask, sink_ref):
    r = q_rows.shape[0]
    low_lanes = lax.broadcasted_iota(jnp.int32, (1, LANES), 1) < HEAD_DIM
    top_rows = lax.broadcasted_iota(jnp.int32, (2 * r, 1), 0) < r
    mask2 = jnp.concatenate([mask, mask], axis=0)
    zero = jnp.zeros((), BF16)
    outs = []
    for j in range(Q_GROUP):
        qj = q_rows[:, j * LANES:(j + 1) * LANES]
        qm = jnp.concatenate([jnp.where(low_lanes, qj, zero), jnp.where(low_lanes, zero, qj)], axis=0)
        s = lax.dot_general(qm, k_all, (((1,), (1,)), ((), ())), preferred_element_type=F32)
        s = jnp.where(mask2, s, -1e30)
        sink = jnp.where(top_rows, sink_ref[j], sink_ref[Q_GROUP + j])
        m = jnp.maximum(jnp.max(s, axis=-1, keepdims=True), sink)
        p = jnp.exp(s - m)
        denom = jnp.sum(p, axis=-1, keepdims=True) + jnp.exp(sink - m)
        o = jnp.dot(p.astype(BF16), v_all, preferred_element_type=F32) / denom
        outs.append(jnp.where(low_lanes, o[:r], o[r:]))
    return jnp.concatenate(outs, axis=1)


def _attn_kernel(sink_ref, q_ref, k_ref, v_ref, qm_ref, km_ref, vm_ref, o_ref, om_ref):
    seq = q_ref.shape[0]
    n_blocks = seq // BLOCK
    band = 3 * BLOCK
    k_head = jnp.concatenate([km_ref[...], k_ref[0:BLOCK - N_META, :]], axis=0)
    v_head = jnp.concatenate([vm_ref[...], v_ref[0:BLOCK - N_META, :]], axis=0)

    n_keys = 2 * BLOCK
    col = lax.broadcasted_iota(jnp.int32, (N_META, n_keys), 1)
    p_idx = lax.broadcasted_iota(jnp.int32, (N_META, n_keys), 0)
    meta_mask = (col < N_META) | (col - p_idx <= WINDOW)
    k_first = jnp.concatenate([km_ref[...], k_ref[0:n_keys - N_META, :]], axis=0)
    v_first = jnp.concatenate([vm_ref[...], v_ref[0:n_keys - N_META, :]], axis=0)
    om_ref[...] = _attend(qm_ref[...], k_first, v_first, meta_mask, sink_ref).astype(BF16)

    col = lax.broadcasted_iota(jnp.int32, (BLOCK, BLOCK + band), 1)
    row = lax.broadcasted_iota(jnp.int32, (BLOCK, BLOCK + band), 0)

    def body(i, carry):
        q_start = pl.multiple_of(BLOCK * i, BLOCK)
        kb = jnp.clip(i - 1, 0, n_blocks - 3)
        k_start = pl.multiple_of(BLOCK * kb, BLOCK)
        rel = (col - BLOCK) + BLOCK * (kb - i) - row
        mask = (col < N_META) | ((col >= BLOCK) & (jnp.abs(rel) <= WINDOW))
        k_all = jnp.concatenate([k_head, k_ref[pl.ds(k_start, band), :]], axis=0)
        v_all = jnp.concatenate([v_head, v_ref[pl.ds(k_start, band), :]], axis=0)
        o = _attend(q_ref[pl.ds(q_start, BLOCK), :], k_all, v_all, mask, sink_ref)
        o_ref[pl.ds(q_start, BLOCK), :] = o.astype(BF16)
        return carry

    lax.fori_loop(0, n_blocks, body, 0)


def _attention(sink, q, k, v, seq):
    n_seq, n_meta = q[0].shape[0], q[1].shape[0]
    per_seq = lambda width: pl.BlockSpec((seq, width), lambda b: (b, 0))
    per_meta = lambda width: pl.BlockSpec((N_META, width), lambda b: (b, 0))
    widths = (ATTN_WIDTH, KV_WIDTH, KV_WIDTH)
    return pl.pallas_call(
        _attn_kernel,
        out_shape=[jax.ShapeDtypeStruct((n_seq, ATTN_WIDTH), BF16), jax.ShapeDtypeStruct((n_meta, ATTN_WIDTH), BF16)],
        grid=(n_seq // seq,),
        in_specs=([pl.BlockSpec(memory_space=pltpu.SMEM)] + [per_seq(w) for w in widths]
                  + [per_meta(w) for w in widths]),
        out_specs=[per_seq(ATTN_WIDTH), per_meta(ATTN_WIDTH)],
        compiler_params=_params(),
        name="attn",
    )(sink, q[0], k[0], v[0], q[1], k[1], v[1])


def _fourier_kernel(f_ref, fm_ref, chan_ref, pos_ref, y_ref, ym_ref, xcs_ref):
    seq_len = N_META + f_ref.shape[0]
    for src, lo, hi in ((fm_ref, 0, N_META), (f_ref, N_META, seq_len)):
        for g in range(N_FOURIER_GROUPS):
            lanes = slice(g * FOURIER_GROUP, (g + 1) * FOURIER_GROUP)
            t = jnp.dot(src[:, lanes], chan_ref[...], preferred_element_type=F32)
            xcs_ref[lo:hi, lanes] = t[:, :FOURIER_GROUP].astype(BF16)
            xcs_ref[seq_len + lo:seq_len + hi, lanes] = t[:, FOURIER_GROUP:].astype(BF16)
    y = jnp.dot(pos_ref[...], xcs_ref[...], preferred_element_type=F32)
    ym_ref[...] = y[:N_META].astype(BF16)
    y_ref[...] = y[N_META:].astype(BF16)


def _fourier(f, chan_t, pos_t, seq):
    n_seq, n_meta = f[0].shape[0], f[1].shape[0]
    seq_len = N_META + seq
    per_seq = pl.BlockSpec((seq, FOURIER_WIDTH), lambda b: (b, 0))
    per_meta = pl.BlockSpec((N_META, FOURIER_WIDTH), lambda b: (b, 0))
    return pl.pallas_call(
        _fourier_kernel,
        out_shape=[jax.ShapeDtypeStruct((n_seq, FOURIER_WIDTH), BF16),
                   jax.ShapeDtypeStruct((n_meta, FOURIER_WIDTH), BF16)],
        grid=(n_seq // seq,),
        in_specs=[per_seq, per_meta, _const_spec((FOURIER_GROUP, 2 * FOURIER_GROUP)),
                  _const_spec((seq_len, 2 * seq_len))],
        out_specs=[per_seq, per_meta],
        scratch_shapes=[pltpu.VMEM((2 * seq_len, FOURIER_WIDTH), BF16)],
        compiler_params=_params(),
        name="fourier",
    )(f[0], f[1], chan_t, pos_t)


def _mixout_body(in_refs, const_refs, out_refs, scratch):
    yf_ref, ya_ref, g_ref, h_ref = in_refs
    wf_ref, wa_ref, wo_ref, gain_ref = const_refs
    o_ref, = out_refs
    y_four = jnp.dot(yf_ref[...], wf_ref[...], preferred_element_type=F32)
    y_attn = jnp.dot(ya_ref[...], wa_ref[...], preferred_element_type=F32)
    g_four = jax.nn.sigmoid(g_ref[:, :D_MODEL].astype(F32))
    g_attn = jax.nn.sigmoid(g_ref[:, D_MODEL:].astype(F32))
    mixed = (g_four * y_four + g_attn * y_attn).astype(BF16)
    out = jnp.dot(mixed, wo_ref[...], preferred_element_type=F32)
    o_ref[...] = h_ref[...] + _rms(out, gain_ref[...])


def _mixout(yf, ya, g, h, wf, wa, wo, gain):
    seq_ins = [(yf[0], None), (ya[0], None), (g[0], None), (h[0], None)]
    out_seq, out_meta = _token_call(_mixout_body, "mixout", seq_ins, [yf[1], ya[1], g[1], h[1]],
                                    [wf, wa, wo, gain], (D_MODEL,), F32)
    return out_seq[0], out_meta[0]


def _ffn_body(in_refs, const_refs, out_refs, scratch):
    h_ref, = in_refs
    gain_pre_ref, wg_ref, wu_ref, wd_ref, gain_post_ref = const_refs
    o_ref, = out_refs
    act_ref, = scratch
    rows = h_ref.shape[0]
    h = h_ref[...]
    u = _rms(h, gain_pre_ref[...]).astype(BF16)
    for lo, hi in FF_CHUNKS:
        gate = jnp.dot(u, wg_ref[:, lo:hi], preferred_element_type=F32)
        up = jnp.dot(u, wu_ref[:, lo:hi], preferred_element_type=F32)
        act_ref[0:rows, lo:hi] = (gate * jax.nn.sigmoid(gate) * up).astype(BF16)
    ff = jnp.dot(act_ref[0:rows, :], wd_ref[...], preferred_element_type=F32)
    o_ref[...] = h + _rms(ff, gain_post_ref[...])


def _ffn(h, gain_pre, wg, wu, wd, gain_post):
    out_seq, out_meta = _token_call(_ffn_body, "ffn", [(h[0], None)], [h[1]],
                                    [gain_pre, wg, wu, wd, gain_post], (D_MODEL,), F32,
                                    scratch_shapes=[pltpu.VMEM((ROW_TILE, D_FF), BF16)])
    return out_seq[0], out_meta[0]


def _rope_tables(positions):
    d = jnp.arange(LANES) % HEAD_DIM
    inv_freq = ROPE_THETA ** (-(2 * (d % (HEAD_DIM // 2))).astype(F32) / HEAD_DIM)
    ang = positions.astype(F32)[:, None] * inv_freq[None, :]
    sign = jnp.where(d < HEAD_DIM // 2, -1.0, 1.0).astype(F32)
    return jnp.cos(ang), jnp.sin(ang) * sign[None, :]


def _dft_tables(n):
    idx = jnp.arange(n, dtype=jnp.int32)
    ang = ((idx[:, None] * idx[None, :]) % n).astype(F32) * (2.0 * np.pi / n)
    scale = n ** -0.5
    return jnp.cos(ang) * scale, jnp.sin(ang) * scale


def _pair_heads(w, axis):
    shape = w.shape
    split = shape[:axis] + (N_KV_HEADS, Q_GROUP, HEAD_DIM) + shape[axis + 1:]
    return jnp.swapaxes(w.reshape(split), axis, axis + 1).reshape(shape)


def kernel(x, meta_tokens, w_in, w_fourier_out, w_attn_out, w_o, sink_logits, norm_mix_pre, norm_mix_post,
           norm_ffn_pre, norm_ffn_post, w_ffn_gate, w_ffn_up, w_ffn_down):
    batch, seq, _ = x.shape
    seq_len = N_META + seq
    depth = w_in.shape[0]
    assert seq % ROW_TILE == 0 and seq // BLOCK >= 3

    h = (x.reshape(batch * seq, D_MODEL),
         jnp.broadcast_to(meta_tokens[None].astype(x.dtype), (batch, N_META, D_MODEL)).reshape(-1, D_MODEL))

    rope_seq = _rope_tables(N_META + jnp.arange(seq))
    rope_meta = _rope_tables(jnp.arange(batch * N_META) % N_META)
    chan_c, chan_s = _dft_tables(FOURIER_GROUP)
    chan_t = jnp.concatenate([chan_c, chan_s], axis=1).astype(BF16)
    pos_c, pos_s = _dft_tables(seq_len)
    pos_t = jnp.concatenate([pos_c, -pos_s], axis=1).astype(BF16)

    gain = lambda t, l: t[l].reshape(1, D_MODEL)

    for l in range(depth):
        w_l = jnp.concatenate([_pair_heads(w_in[l, :, :ATTN_WIDTH], 1), w_in[l, :, ATTN_WIDTH:]], axis=1)
        (q, k, v, f, g), (qm, km, vm, fm, gm) = _inproj(h, gain(norm_mix_pre, l), w_l.astype(BF16),
                                                        rope_seq, rope_meta, seq)
        ya = _attention(sink_logits[l], (q, qm), (k, km), (v, vm), seq)
        yf = _fourier((f, fm), chan_t, pos_t, seq)
        h = _mixout(yf, ya, (g, gm), h, w_fourier_out[l].astype(BF16),
                    _pair_heads(w_attn_out[l], 0).astype(BF16), w_o[l].astype(BF16), gain(norm_mix_post, l))
        h = _ffn(h, gain(norm_ffn_pre, l), w_ffn_gate[l].astype(BF16), w_ffn_up[l].astype(BF16),
                 w_ffn_down[l].astype(BF16), gain(norm_ffn_post, l))

    return h[0].reshape(batch, seq, D_MODEL)
```

```python
import functools

import jax
import jax.numpy as jnp
import numpy as np
from jax import lax
from jax.experimental import pallas as pl
from jax.experimental.pallas import tpu as pltpu

D_MODEL = 1024
N_META = 16
HEAD_DIM = 64
N_HEADS = 8
N_KV_HEADS = 2
Q_GROUP = N_HEADS // N_KV_HEADS
ATTN_WIDTH = N_HEADS * HEAD_DIM
KV_WIDTH = N_KV_HEADS * HEAD_DIM
FOURIER_WIDTH = D_MODEL - ATTN_WIDTH
FOURIER_GROUP = 128
N_FOURIER_GROUPS = FOURIER_WIDTH // FOURIER_GROUP
GATE_WIDTH = 2 * D_MODEL
IN_WIDTH = ATTN_WIDTH + 2 * KV_WIDTH + FOURIER_WIDTH + GATE_WIDTH
PROJ_WIDTHS = (ATTN_WIDTH, KV_WIDTH, KV_WIDTH, FOURIER_WIDTH, GATE_WIDTH)
WINDOW = 128
BLOCK = 128
ROPE_THETA = 10000.0
D_FF = 2816
EPS = 1e-6

LANES = 128
VMEM_LIMIT = 56 * 1024 * 1024
ROW_TILE = 1024
FF_CHUNKS = ((0, 1024), (1024, 2048), (2048, D_FF))

BF16 = jnp.bfloat16
F32 = jnp.float32


def _const_spec(shape):
    return pl.BlockSpec(shape, lambda *_: (0,) * len(shape), pipeline_mode=pl.Buffered(1))


def _params():
    return pltpu.CompilerParams(dimension_semantics=("arbitrary",), vmem_limit_bytes=VMEM_LIMIT)


def _rms(x, gain):
    return x * lax.rsqrt(jnp.mean(x * x, axis=-1, keepdims=True) + EPS) * gain


def _token_call(body, name, seq_ins, meta_ins, consts, out_widths, out_dtype, scratch_shapes=()):
    n_seq = seq_ins[0][0].shape[0]
    n_meta = meta_ins[0].shape[0]
    n_tiles = n_seq // ROW_TILE
    n_in, n_const, n_out = len(seq_ins), len(consts), len(out_widths)

    def kernel(*refs):
        seq_refs, refs = refs[:n_in], refs[n_in:]
        meta_refs, refs = refs[:n_in], refs[n_in:]
        const_refs, refs = refs[:n_const], refs[n_const:]
        out_seq, refs = refs[:n_out], refs[n_out:]
        out_meta, scratch = refs[:n_out], refs[n_out:]
        step = pl.program_id(0)

        @pl.when(step < n_tiles)
        def _():
            body(seq_refs, const_refs, out_seq, scratch)

        @pl.when(step == n_tiles)
        def _():
            body(meta_refs, const_refs, out_meta, scratch)

    last = n_tiles - 1
    seq_specs = []
    for arr, index_fn in seq_ins:
        index_fn = index_fn or (lambda i: i)
        seq_specs.append(pl.BlockSpec((ROW_TILE, arr.shape[1]),
                                      functools.partial(lambda fn, i: (fn(jnp.minimum(i, last)), 0), index_fn)))
    meta_specs = [pl.BlockSpec(arr.shape, lambda i: (0, 0)) for arr in meta_ins]
    const_specs = [_const_spec(c.shape) for c in consts]
    out_shape = ([jax.ShapeDtypeStruct((n_seq, w), out_dtype) for w in out_widths]
                 + [jax.ShapeDtypeStruct((n_meta, w), out_dtype) for w in out_widths])
    out_specs = ([pl.BlockSpec((ROW_TILE, w), lambda i: (jnp.minimum(i, last), 0)) for w in out_widths]
                 + [pl.BlockSpec((n_meta, w), lambda i: (0, 0)) for w in out_widths])
    outs = pl.pallas_call(
        kernel,
        out_shape=out_shape,
        grid=(n_tiles + 1,),
        in_specs=seq_specs + meta_specs + const_specs,
        out_specs=out_specs,
        scratch_shapes=list(scratch_shapes),
        compiler_params=_params(),
        name=name,
    )(*[a for a, _ in seq_ins], *meta_ins, *consts)
    return outs[:n_out], outs[n_out:]


def _inproj_body(in_refs, const_refs, out_refs, scratch):
    h_ref, cos_ref, sin_ref = in_refs
    gain_ref, w_ref = const_refs
    q_ref, k_ref, v_ref, f_ref, g_ref = out_refs
    u = _rms(h_ref[...], gain_ref[...]).astype(BF16)
    cos = cos_ref[...]
    sin = sin_ref[...]
    first_half = (lax.broadcasted_iota(jnp.int32, (1, LANES), 1) % HEAD_DIM) < HEAD_DIM // 2

    def rope(y):
        rot = jnp.where(first_half, pltpu.roll(y, LANES - HEAD_DIM // 2, 1),
                        pltpu.roll(y, HEAD_DIM // 2, 1))
        return y * cos + rot * sin

    def proj(lo, hi):
        return jnp.dot(u, w_ref[:, lo:hi], preferred_element_type=F32)

    q = proj(0, ATTN_WIDTH)
    scale = HEAD_DIM ** -0.5
    for c in range(ATTN_WIDTH // LANES):
        q_ref[:, c * LANES:(c + 1) * LANES] = (rope(q[:, c * LANES:(c + 1) * LANES]) * scale).astype(BF16)
    kv = proj(ATTN_WIDTH, ATTN_WIDTH + 2 * KV_WIDTH)
    k_ref[...] = rope(kv[:, :KV_WIDTH]).astype(BF16)
    v_ref[...] = kv[:, KV_WIDTH:].astype(BF16)
    f0 = ATTN_WIDTH + 2 * KV_WIDTH
    f_ref[...] = proj(f0, f0 + FOURIER_WIDTH).astype(BF16)
    g0 = f0 + FOURIER_WIDTH
    for c in range(GATE_WIDTH // 1024):
        g_ref[:, c * 1024:(c + 1) * 1024] = proj(g0 + c * 1024, g0 + (c + 1) * 1024).astype(BF16)


def _inproj(h, gain, w, rope_seq, rope_meta, seq):
    tiles_per_seq = seq // ROW_TILE
    table_index = lambda i: i % tiles_per_seq
    seq_ins = [(h[0], None), (rope_seq[0], table_index), (rope_seq[1], table_index)]
    return _token_call(_inproj_body, "inproj", seq_ins, [h[1], rope_meta[0], rope_meta[1]], [gain, w],
                       PROJ_WIDTHS, BF16)


MASKED = -1e30


def _attend(q_rows, k_all, v_ones, bias, sink_ref):
    r = q_rows.shape[0]
    low_lanes = lax.broadcasted_iota(jnp.int32, (1, LANES), 1) < HEAD_DIM
    top_rows = lax.broadcasted_iota(jnp.int32, (2 * r, 1), 0) < r
    bias2 = jnp.concatenate([bias, bias], axis=0)
    zero = jnp.zeros((), BF16)
    outs = []
    for j in range(Q_GROUP):
        qj = q_rows[:, j * LANES:(j + 1) * LANES]
        qm = jnp.concatenate([jnp.where(low_lanes, qj, zero), jnp.where(low_lanes, zero, qj)], axis=0)
        s = lax.dot_general(qm, k_all, (((1,), (1,)), ((), ())), preferred_element_type=F32) + bias2
        sink = jnp.where(top_rows, sink_ref[j], sink_ref[Q_GROUP + j])
        m = jnp.maximum(jnp.max(s, axis=-1, keepdims=True), sink)
        p = jnp.exp((s - m).astype(BF16))
        o = jnp.dot(p, v_ones, preferred_element_type=F32)
        o = o[:, :LANES] / (o[:, LANES:] + jnp.exp(sink - m))
        outs.append(jnp.where(low_lanes, o[:r], o[r:]))
    return jnp.concatenate(outs, axis=1)


def _attn_kernel(sink_ref, q_ref, k_ref, v_ref, qm_ref, km_ref, vm_ref, bias_ref, bias_meta_ref, o_ref, om_ref):
    seq = q_ref.shape[0]
    n_blocks = seq // BLOCK
    band = 3 * BLOCK

    def with_ones(v):
        return jnp.concatenate([v, jnp.ones_like(v)], axis=1)

    n_first = bias_meta_ref.shape[1] - N_META
    k_first = jnp.concatenate([km_ref[...], k_ref[0:n_first, :]], axis=0)
    v_first = jnp.concatenate([vm_ref[...], v_ref[0:n_first, :]], axis=0)
    om_ref[...] = _attend(qm_ref[...], k_first, with_ones(v_first), bias_meta_ref[...], sink_ref).astype(BF16)

    k_head = k_first[0:BLOCK]
    v_head = v_first[0:BLOCK]

    def body(i, carry):
        q_start = pl.multiple_of(BLOCK * i, BLOCK)
        kb = jnp.clip(i - 1, 0, n_blocks - 3)
        k_start = pl.multiple_of(BLOCK * kb, BLOCK)
        kind = jnp.where(i == 0, 0, jnp.where(i == n_blocks - 1, 2, 1))
        k_all = jnp.concatenate([k_head, k_ref[pl.ds(k_start, band), :]], axis=0)
        v_all = jnp.concatenate([v_head, v_ref[pl.ds(k_start, band), :]], axis=0)
        o = _attend(q_ref[pl.ds(q_start, BLOCK), :], k_all, with_ones(v_all), bias_ref[kind], sink_ref)
        o_ref[pl.ds(q_start, BLOCK), :] = o.astype(BF16)
        return carry

    lax.fori_loop(0, n_blocks, body, 0, unroll=4)


def _attention_bias():
    row = np.arange(BLOCK)[:, None]
    col = np.arange(BLOCK + 3 * BLOCK)[None, :]
    kinds = []
    for first_band_block in (0, -1, -2):
        rel = (col - BLOCK) + BLOCK * first_band_block - row
        kinds.append((col < N_META) | ((col >= BLOCK) & (np.abs(rel) <= WINDOW)))
    p_idx = np.arange(N_META)[:, None]
    col = np.arange(2 * BLOCK)[None, :]
    meta = (col < N_META) | (col - p_idx <= WINDOW)
    to_bias = lambda visible: np.where(visible, 0.0, MASKED).astype(np.float32)
    return to_bias(np.stack(kinds)), to_bias(meta)


def _attention(sink, q, k, v, seq):
    n_seq, n_meta = q[0].shape[0], q[1].shape[0]
    bias, bias_meta = _attention_bias()
    per_seq = lambda width: pl.BlockSpec((seq, width), lambda b: (b, 0))
    per_meta = lambda width: pl.BlockSpec((N_META, width), lambda b: (b, 0))
    widths = (ATTN_WIDTH, KV_WIDTH, KV_WIDTH)
    return pl.pallas_call(
        _attn_kernel,
        out_shape=[jax.ShapeDtypeStruct((n_seq, ATTN_WIDTH), BF16), jax.ShapeDtypeStruct((n_meta, ATTN_WIDTH), BF16)],
        grid=(n_seq // seq,),
        in_specs=([pl.BlockSpec(memory_space=pltpu.SMEM)] + [per_seq(w) for w in widths]
                  + [per_meta(w) for w in widths] + [_const_spec(bias.shape), _const_spec(bias_meta.shape)]),
        out_specs=[per_seq(ATTN_WIDTH), per_meta(ATTN_WIDTH)],
        compiler_params=_params(),
        name="attn",
    )(sink, q[0], k[0], v[0], q[1], k[1], v[1], bias, bias_meta)


def _fourier_kernel(f_ref, fm_ref, chan_ref, pos_ref, y_ref, ym_ref, xcs_ref):
    seq_len = N_META + f_ref.shape[0]
    for src, lo, hi in ((fm_ref, 0, N_META), (f_ref, N_META, seq_len)):
        for g in range(N_FOURIER_GROUPS):
            lanes = slice(g * FOURIER_GROUP, (g + 1) * FOURIER_GROUP)
            t = jnp.dot(src[:, lanes], chan_ref[...], preferred_element_type=F32)
            xcs_ref[lo:hi, lanes] = t[:, :FOURIER_GROUP].astype(BF16)
            xcs_ref[seq_len + lo:seq_len + hi, lanes] = t[:, FOURIER_GROUP:].astype(BF16)
    y = jnp.dot(pos_ref[...], xcs_ref[...], preferred_element_type=F32)
    ym_ref[...] = y[:N_META].astype(BF16)
    y_ref[...] = y[N_META:].astype(BF16)


def _fourier(f, chan_t, pos_t, seq):
    n_seq, n_meta = f[0].shape[0], f[1].shape[0]
    seq_len = N_META + seq
    per_seq = pl.BlockSpec((seq, FOURIER_WIDTH), lambda b: (b, 0))
    per_meta = pl.BlockSpec((N_META, FOURIER_WIDTH), lambda b: (b, 0))
    return pl.pallas_call(
        _fourier_kernel,
        out_shape=[jax.ShapeDtypeStruct((n_seq, FOURIER_WIDTH), BF16),
                   jax.ShapeDtypeStruct((n_meta, FOURIER_WIDTH), BF16)],
        grid=(n_seq // seq,),
        in_specs=[per_seq, per_meta, _const_spec((FOURIER_GROUP, 2 * FOURIER_GROUP)),
                  _const_spec((seq_len, 2 * seq_len))],
        out_specs=[per_seq, per_meta],
        scratch_shapes=[pltpu.VMEM((2 * seq_len, FOURIER_WIDTH), BF16)],
        compiler_params=_params(),
        name="fourier",
    )(f[0], f[1], chan_t, pos_t)


def _mixout_body(in_refs, const_refs, out_refs, scratch):
    yf_ref, ya_ref, g_ref, h_ref = in_refs
    wf_ref, wa_ref, wo_ref, gain_ref = const_refs
    o_ref, = out_refs
    y_four = jnp.dot(yf_ref[...], wf_ref[...], preferred_element_type=F32)
    y_attn = jnp.dot(ya_ref[...], wa_ref[...], preferred_element_type=F32)
    g_four = jax.nn.sigmoid(g_ref[:, :D_MODEL].astype(F32))
    g_attn = jax.nn.sigmoid(g_ref[:, D_MODEL:].astype(F32))
    mixed = (g_four * y_four + g_attn * y_attn).astype(BF16)
    out = jnp.dot(mixed, wo_ref[...], preferred_element_type=F32)
    o_ref[...] = h_ref[...] + _rms(out, gain_ref[...])


def _mixout(yf, ya, g, h, wf, wa, wo, gain):
    seq_ins = [(yf[0], None), (ya[0], None), (g[0], None), (h[0], None)]
    out_seq, out_meta = _token_call(_mixout_body, "mixout", seq_ins, [yf[1], ya[1], g[1], h[1]],
                                    [wf, wa, wo, gain], (D_MODEL,), F32)
    return out_seq[0], out_meta[0]


def _ffn_body(in_refs, const_refs, out_refs, scratch):
    h_ref, = in_refs
    gain_pre_ref, wg_ref, wu_ref, wd_ref, gain_post_ref = const_refs
    o_ref, = out_refs
    act_ref, = scratch
    rows = h_ref.shape[0]
    h = h_ref[...]
    u = _rms(h, gain_pre_ref[...]).astype(BF16)
    for lo, hi in FF_CHUNKS:
        gate = jnp.dot(u, wg_ref[:, lo:hi], preferred_element_type=F32)
        up = jnp.dot(u, wu_ref[:, lo:hi], preferred_element_type=F32)
        act_ref[0:rows, lo:hi] = (gate * jax.nn.sigmoid(gate) * up).astype(BF16)
    ff = jnp.dot(act_ref[0:rows, :], wd_ref[...], preferred_element_type=F32)
    o_ref[...] = h + _rms(ff, gain_post_ref[...])


def _ffn(h, gain_pre, wg, wu, wd, gain_post):
    out_seq, out_meta = _token_call(_ffn_body, "ffn", [(h[0], None)], [h[1]],
                                    [gain_pre, wg, wu, wd, gain_post], (D_MODEL,), F32,
                                    scratch_shapes=[pltpu.VMEM((ROW_TILE, D_FF), BF16)])
    return out_seq[0], out_meta[0]


def _rope_tables(positions):
    d = jnp.arange(LANES) % HEAD_DIM
    inv_freq = ROPE_THETA ** (-(2 * (d % (HEAD_DIM // 2))).astype(F32) / HEAD_DIM)
    ang = positions.astype(F32)[:, None] * inv_freq[None, :]
    sign = jnp.where(d < HEAD_DIM // 2, -1.0, 1.0).astype(F32)
    return jnp.cos(ang), jnp.sin(ang) * sign[None, :]


def _dft_tables(n):
    idx = np.arange(n, dtype=np.int64)
    ang = ((idx[:, None] * idx[None, :]) % n) * (2.0 * np.pi / n)
    scale = n ** -0.5
    return (np.cos(ang) * scale).astype(np.float32), (np.sin(ang) * scale).astype(np.float32)


def _pair_heads(w, axis):
    shape = w.shape
    split = shape[:axis] + (N_KV_HEADS, Q_GROUP, HEAD_DIM) + shape[axis + 1:]
    return jnp.swapaxes(w.reshape(split), axis, axis + 1).reshape(shape)


def kernel(x, meta_tokens, w_in, w_fourier_out, w_attn_out, w_o, sink_logits, norm_mix_pre, norm_mix_post,
           norm_ffn_pre, norm_ffn_post, w_ffn_gate, w_ffn_up, w_ffn_down):
    batch, seq, _ = x.shape
    seq_len = N_META + seq
    depth = w_in.shape[0]
    assert seq % ROW_TILE == 0 and seq // BLOCK >= 3

    h = (x.reshape(batch * seq, D_MODEL),
         jnp.broadcast_to(meta_tokens[None].astype(x.dtype), (batch, N_META, D_MODEL)).reshape(-1, D_MODEL))

    rope_seq = _rope_tables(N_META + jnp.arange(seq))
    rope_meta = _rope_tables(jnp.arange(batch * N_META) % N_META)
    chan_c, chan_s = _dft_tables(FOURIER_GROUP)
    chan_t = jnp.asarray(np.concatenate([chan_c, chan_s], axis=1)).astype(BF16)
    pos_c, pos_s = _dft_tables(seq_len)
    pos_t = jnp.asarray(np.concatenate([pos_c, -pos_s], axis=1)).astype(BF16)

    gain = lambda t, l: t[l].reshape(1, D_MODEL)

    for l in range(depth):
        w_l = jnp.concatenate([_pair_heads(w_in[l, :, :ATTN_WIDTH], 1), w_in[l, :, ATTN_WIDTH:]], axis=1)
        (q, k, v, f, g), (qm, km, vm, fm, gm) = _inproj(h, gain(norm_mix_pre, l), w_l.astype(BF16),
                                                        rope_seq, rope_meta, seq)
        ya = _attention(sink_logits[l], (q, qm), (k, km), (v, vm), seq)
        yf = _fourier((f, fm), chan_t, pos_t, seq)
        h = _mixout(yf, ya, (g, gm), h, w_fourier_out[l].astype(BF16),
                    _pair_heads(w_attn_out[l], 0).astype(BF16), w_o[l].astype(BF16), gain(norm_mix_post, l))
        h = _ffn(h, gain(norm_ffn_pre, l), w_ffn_gate[l].astype(BF16), w_ffn_up[l].astype(BF16),
                 w_ffn_down[l].astype(BF16), gain(norm_ffn_post, l))

    return h[0].reshape(batch, seq, D_MODEL)
```

```python
import functools

import jax
import jax.numpy as jnp
import numpy as np
from jax import lax
from jax.experimental import pallas as pl
from jax.experimental.pallas import tpu as pltpu

D_MODEL = 1024
N_META = 16
HEAD_DIM = 64
N_HEADS = 8
N_KV_HEADS = 2
Q_GROUP = N_HEADS // N_KV_HEADS
ATTN_WIDTH = N_HEADS * HEAD_DIM
KV_WIDTH = N_KV_HEADS * HEAD_DIM
FOURIER_WIDTH = D_MODEL - ATTN_WIDTH
FOURIER_GROUP = 128
N_FOURIER_GROUPS = FOURIER_WIDTH // FOURIER_GROUP
GATE_WIDTH = 2 * D_MODEL
IN_WIDTH = ATTN_WIDTH + 2 * KV_WIDTH + FOURIER_WIDTH + GATE_WIDTH
PROJ_WIDTHS = (ATTN_WIDTH, KV_WIDTH, KV_WIDTH, FOURIER_WIDTH, GATE_WIDTH)
WINDOW = 128
BLOCK = 128
ROPE_THETA = 10000.0
D_FF = 2816
EPS = 1e-6

LANES = 128
VMEM_LIMIT = 56 * 1024 * 1024
ROW_TILE = 1024
FF_CHUNKS = ((0, 1024), (1024, 2048), (2048, D_FF))

BF16 = jnp.bfloat16
F32 = jnp.float32


def _const_spec(shape):
    return pl.BlockSpec(shape, lambda *_: (0,) * len(shape), pipeline_mode=pl.Buffered(1))


def _params():
    return pltpu.CompilerParams(dimension_semantics=("arbitrary",), vmem_limit_bytes=VMEM_LIMIT)


def _rms(x, gain):
    return x * lax.rsqrt(jnp.mean(x * x, axis=-1, keepdims=True) + EPS) * gain


def _token_call(body, name, seq_ins, meta_ins, consts, out_widths, out_dtype, scratch_shapes=()):
    n_seq = seq_ins[0][0].shape[0]
    n_meta = meta_ins[0].shape[0]
    n_tiles = n_seq // ROW_TILE
    n_in, n_const, n_out = len(seq_ins), len(consts), len(out_widths)

    def kernel(*refs):
        seq_refs, refs = refs[:n_in], refs[n_in:]
        meta_refs, refs = refs[:n_in], refs[n_in:]
        const_refs, refs = refs[:n_const], refs[n_const:]
        out_seq, refs = refs[:n_out], refs[n_out:]
        out_meta, scratch = refs[:n_out], refs[n_out:]
        step = pl.program_id(0)

        @pl.when(step < n_tiles)
        def _():
            body(seq_refs, const_refs, out_seq, scratch)

        @pl.when(step == n_tiles)
        def _():
            body(meta_refs, const_refs, out_meta, scratch)

    last = n_tiles - 1
    seq_specs = []
    for arr, index_fn in seq_ins:
        index_fn = index_fn or (lambda i: i)
        seq_specs.append(pl.BlockSpec((ROW_TILE, arr.shape[1]),
                                      functools.partial(lambda fn, i: (fn(jnp.minimum(i, last)), 0), index_fn)))
    meta_specs = [pl.BlockSpec(arr.shape, lambda i: (0, 0)) for arr in meta_ins]
    const_specs = [_const_spec(c.shape) for c in consts]
    out_shape = ([jax.ShapeDtypeStruct((n_seq, w), out_dtype) for w in out_widths]
                 + [jax.ShapeDtypeStruct((n_meta, w), out_dtype) for w in out_widths])
    out_specs = ([pl.BlockSpec((ROW_TILE, w), lambda i: (jnp.minimum(i, last), 0)) for w in out_widths]
                 + [pl.BlockSpec((n_meta, w), lambda i: (0, 0)) for w in out_widths])
    outs = pl.pallas_call(
        kernel,
        out_shape=out_shape,
        grid=(n_tiles + 1,),
        in_specs=seq_specs + meta_specs + const_specs,
        out_specs=out_specs,
        scratch_shapes=list(scratch_shapes),
        compiler_params=_params(),
        name=name,
    )(*[a for a, _ in seq_ins], *meta_ins, *consts)
    return outs[:n_out], outs[n_out:]


def _inproj_body(in_refs, const_refs, out_refs, scratch):
    h_ref, cos_ref, sin_ref = in_refs
    gain_ref, w_ref = const_refs
    q_ref, k_ref, v_ref, f_ref, g_ref = out_refs
    u = _rms(h_ref[...], gain_ref[...]).astype(BF16)
    cos = cos_ref[...]
    sin = sin_ref[...]
    first_half = (lax.broadcasted_iota(jnp.int32, (1, LANES), 1) % HEAD_DIM) < HEAD_DIM // 2

    def rope(y):
        rot = jnp.where(first_half, pltpu.roll(y, LANES - HEAD_DIM // 2, 1),
                        pltpu.roll(y, HEAD_DIM // 2, 1))
        return y * cos + rot * sin

    def proj(lo, hi):
        return jnp.dot(u, w_ref[:, lo:hi], preferred_element_type=F32)

    q = proj(0, ATTN_WIDTH)
    scale = HEAD_DIM ** -0.5
    for c in range(ATTN_WIDTH // LANES):
        q_ref[:, c * LANES:(c + 1) * LANES] = (rope(q[:, c * LANES:(c + 1) * LANES]) * scale).astype(BF16)
    kv = proj(ATTN_WIDTH, ATTN_WIDTH + 2 * KV_WIDTH)
    k_ref[...] = rope(kv[:, :KV_WIDTH]).astype(BF16)
    v_ref[...] = kv[:, KV_WIDTH:].astype(BF16)
    f0 = ATTN_WIDTH + 2 * KV_WIDTH
    f_ref[...] = proj(f0, f0 + FOURIER_WIDTH).astype(BF16)
    g0 = f0 + FOURIER_WIDTH
    for c in range(GATE_WIDTH // 1024):
        g_ref[:, c * 1024:(c + 1) * 1024] = proj(g0 + c * 1024, g0 + (c + 1) * 1024).astype(BF16)


def _inproj(h, gain, w, rope_seq, rope_meta, seq):
    tiles_per_seq = seq // ROW_TILE
    table_index = lambda i: i % tiles_per_seq
    seq_ins = [(h[0], None), (rope_seq[0], table_index), (rope_seq[1], table_index)]
    return _token_call(_inproj_body, "inproj", seq_ins, [h[1], rope_meta[0], rope_meta[1]], [gain, w],
                       PROJ_WIDTHS, BF16)


MASKED = -1e30


def _attend(q_rows, k_all, v_ones, bias, sink_ref):
    r = q_rows.shape[0]
    low_lanes = lax.broadcasted_iota(jnp.int32, (1, LANES), 1) < HEAD_DIM
    top_rows = lax.broadcasted_iota(jnp.int32, (2 * r, 1), 0) < r
    bias2 = jnp.concatenate([bias, bias], axis=0)
    zero = jnp.zeros((), BF16)
    outs = []
    for j in range(Q_GROUP):
        qj = q_rows[:, j * LANES:(j + 1) * LANES]
        qm = jnp.concatenate([jnp.where(low_lanes, qj, zero), jnp.where(low_lanes, zero, qj)], axis=0)
        s = lax.dot_general(qm, k_all, (((1,), (1,)), ((), ())), preferred_element_type=F32) + bias2
        sink = jnp.where(top_rows, sink_ref[j], sink_ref[Q_GROUP + j])
        m = jnp.maximum(jnp.max(s, axis=-1, keepdims=True), sink)
        p = jnp.exp((s - m).astype(BF16))
        o = jnp.dot(p, v_ones, preferred_element_type=F32)
        o = o[:, :LANES] / (o[:, LANES:] + jnp.exp(sink - m))
        outs.append(jnp.where(low_lanes, o[:r], o[r:]))
    return jnp.concatenate(outs, axis=1)


def _attn_kernel(sink_ref, q_ref, k_ref, v_ref, qm_ref, km_ref, vm_ref, bias_ref, bias_meta_ref, o_ref, om_ref):
    seq = q_ref.shape[0]
    n_blocks = seq // BLOCK
    band = 3 * BLOCK

    def with_ones(v):
        return jnp.concatenate([v, jnp.ones_like(v)], axis=1)

    n_first = bias_meta_ref.shape[1] - N_META
    k_first = jnp.concatenate([km_ref[...], k_ref[0:n_first, :]], axis=0)
    v_first = jnp.concatenate([vm_ref[...], v_ref[0:n_first, :]], axis=0)
    om_ref[...] = _attend(qm_ref[...], k_first, with_ones(v_first), bias_meta_ref[...], sink_ref).astype(BF16)

    k_head = k_first[0:BLOCK]
    v_head = v_first[0:BLOCK]

    def body(i, carry):
        q_start = pl.multiple_of(BLOCK * i, BLOCK)
        kb = jnp.clip(i - 1, 0, n_blocks - 3)
        k_start = pl.multiple_of(BLOCK * kb, BLOCK)
        kind = jnp.where(i == 0, 0, jnp.where(i == n_blocks - 1, 2, 1))
        k_all = jnp.concatenate([k_head, k_ref[pl.ds(k_start, band), :]], axis=0)
        v_all = jnp.concatenate([v_head, v_ref[pl.ds(k_start, band), :]], axis=0)
        o = _attend(q_ref[pl.ds(q_start, BLOCK), :], k_all, with_ones(v_all), bias_ref[kind], sink_ref)
        o_ref[pl.ds(q_start, BLOCK), :] = o.astype(BF16)
        return carry

    lax.fori_loop(0, n_blocks, body, 0, unroll=4)


def _attention_bias():
    row = np.arange(BLOCK)[:, None]
    col = np.arange(BLOCK + 3 * BLOCK)[None, :]
    kinds = []
    for first_band_block in (0, -1, -2):
        rel = (col - BLOCK) + BLOCK * first_band_block - row
        kinds.append((col < N_META) | ((col >= BLOCK) & (np.abs(rel) <= WINDOW)))
    p_idx = np.arange(N_META)[:, None]
    col = np.arange(2 * BLOCK)[None, :]
    meta = (col < N_META) | (col - p_idx <= WINDOW)
    to_bias = lambda visible: np.where(visible, 0.0, MASKED).astype(np.float32)
    return to_bias(np.stack(kinds)), to_bias(meta)


def _attention(sink, q, k, v, seq):
    n_seq, n_meta = q[0].shape[0], q[1].shape[0]
    bias, bias_meta = _attention_bias()
    per_seq = lambda width: pl.BlockSpec((seq, width), lambda b: (b, 0))
    per_meta = lambda width: pl.BlockSpec((N_META, width), lambda b: (b, 0))
    widths = (ATTN_WIDTH, KV_WIDTH, KV_WIDTH)
    return pl.pallas_call(
        _attn_kernel,
        out_shape=[jax.ShapeDtypeStruct((n_seq, ATTN_WIDTH), BF16), jax.ShapeDtypeStruct((n_meta, ATTN_WIDTH), BF16)],
        grid=(n_seq // seq,),
        in_specs=([pl.BlockSpec(memory_space=pltpu.SMEM)] + [per_seq(w) for w in widths]
                  + [per_meta(w) for w in widths] + [_const_spec(bias.shape), _const_spec(bias_meta.shape)]),
        out_specs=[per_seq(ATTN_WIDTH), per_meta(ATTN_WIDTH)],
        compiler_params=_params(),
        name="attn",
    )(sink, q[0], k[0], v[0], q[1], k[1], v[1], bias, bias_meta)


XCH = 256
XCH_WIN = XCH + N_META


def _fourier_kernel(f_ref, fm_ref, xch_ref, xch_meta_ref, chan_c_ref, chan_s_ref, cos_ref, sin_ref,
                    y_ref, ym_ref, ex_ref, ox_ref, ec_ref, os_ref, t_ref):
    seq = f_ref.shape[0]
    half = seq // 2
    centre = half - N_META // 2
    f32_dot = functools.partial(jnp.dot, preferred_element_type=F32)

    fm = fm_ref[...].astype(F32)
    gm = f32_dot(xch_meta_ref[...], f_ref[seq - N_META:seq, :])
    ex_ref[0:N_META, :] = (fm + gm).astype(BF16)
    ox_ref[0:N_META, :] = (fm - gm).astype(BF16)
    for c in range(half // XCH):
        w = seq - XCH_WIN - XCH * c
        g = f32_dot(xch_ref[...], f_ref[w:w + XCH_WIN, :])
        fs = f_ref[XCH * c:XCH * (c + 1), :].astype(F32)
        if XCH * (c + 1) > centre:
            paired = XCH * c + lax.broadcasted_iota(jnp.int32, (XCH, 1), 0) < centre
            e = fs + jnp.where(paired, g, 0.0)
            o = jnp.where(paired, fs - g, 0.0)
        else:
            e, o = fs + g, fs - g
        rows = slice(N_META + XCH * c, N_META + XCH * (c + 1))
        ex_ref[rows, :] = e.astype(BF16)
        ox_ref[rows, :] = o.astype(BF16)

    for p in range(FOURIER_WIDTH // (2 * FOURIER_GROUP)):
        lanes = slice(2 * FOURIER_GROUP * p, 2 * FOURIER_GROUP * (p + 1))
        ec_ref[:, lanes] = f32_dot(ex_ref[:, lanes], chan_c_ref[...]).astype(BF16)
        os_ref[:, lanes] = f32_dot(ox_ref[:, lanes], chan_s_ref[...]).astype(BF16)

    a = f32_dot(cos_ref[...], ec_ref[...])
    b = f32_dot(sin_ref[...], os_ref[...])
    u = a - b
    t_ref[...] = (a + b).astype(BF16)
    ym_ref[...] = u[0:N_META].astype(BF16)
    y_ref[0:half - N_META, :] = u[N_META:half].astype(BF16)
    mirrored = f32_dot(xch_meta_ref[...], t_ref[half:half + N_META, :])
    low = lax.broadcasted_iota(jnp.int32, (N_META, 1), 0) < N_META // 2
    y_ref[half - N_META:half, :] = jnp.where(low, u[half:half + N_META], mirrored).astype(BF16)
    for c in range(half // XCH):
        start = half + XCH * c
        w = seq - start - XCH
        y_ref[start:start + XCH, :] = f32_dot(xch_ref[...], t_ref[w:w + XCH_WIN, :]).astype(BF16)


def _fourier_tables(seq):
    n = N_META + seq
    h = n // 2
    rows = N_META + seq // 2
    idx = np.arange(rows, dtype=np.int64)
    ang = ((idx[:, None] * idx[None, :]) % n) * (2.0 * np.pi / n)
    inside = idx <= h
    strict = (idx >= 1) & (idx < h)
    cos_t = np.where(inside[:, None] & inside[None, :], np.cos(ang), 0.0) * n ** -0.5
    sin_t = np.where(strict[:, None] & strict[None, :], np.sin(ang), 0.0) * n ** -0.5
    d = np.arange(FOURIER_GROUP, dtype=np.int64)
    ang = ((d[:, None] * d[None, :]) % FOURIER_GROUP) * (2.0 * np.pi / FOURIER_GROUP)
    pair = np.eye(2)
    chan_c = np.kron(pair, np.cos(ang)) * FOURIER_GROUP ** -0.5
    chan_s = np.kron(pair, np.sin(ang)) * FOURIER_GROUP ** -0.5
    xch = np.zeros((XCH, XCH_WIN))
    xch[np.arange(XCH), XCH - np.arange(XCH)] = 1.0
    xch_meta = np.zeros((N_META, N_META))
    xch_meta[np.arange(1, N_META), N_META - np.arange(1, N_META)] = 1.0
    return [t.astype(np.float32) for t in (xch, xch_meta, chan_c, chan_s, cos_t, sin_t)]


def _fourier(f, tables, seq):
    n_seq, n_meta = f[0].shape[0], f[1].shape[0]
    assert seq % (2 * XCH) == 0
    half_rows = N_META + seq // 2
    per_seq = pl.BlockSpec((seq, FOURIER_WIDTH), lambda b: (b, 0))
    per_meta = pl.BlockSpec((N_META, FOURIER_WIDTH), lambda b: (b, 0))
    return pl.pallas_call(
        _fourier_kernel,
        out_shape=[jax.ShapeDtypeStruct((n_seq, FOURIER_WIDTH), BF16),
                   jax.ShapeDtypeStruct((n_meta, FOURIER_WIDTH), BF16)],
        grid=(n_seq // seq,),
        in_specs=[per_seq, per_meta] + [_const_spec(t.shape) for t in tables],
        out_specs=[per_seq, per_meta],
        scratch_shapes=[pltpu.VMEM((half_rows, FOURIER_WIDTH), BF16)] * 5,
        compiler_params=_params(),
        name="fourier",
    )(f[0], f[1], *tables)


def _mixout_body(in_refs, const_refs, out_refs, scratch):
    yf_ref, ya_ref, g_ref, h_ref = in_refs
    wf_ref, wa_ref, wo_ref, gain_ref = const_refs
    o_ref, = out_refs
    y_four = jnp.dot(yf_ref[...], wf_ref[...], preferred_element_type=F32)
    y_attn = jnp.dot(ya_ref[...], wa_ref[...], preferred_element_type=F32)
    g_four = jax.nn.sigmoid(g_ref[:, :D_MODEL].astype(F32))
    g_attn = jax.nn.sigmoid(g_ref[:, D_MODEL:].astype(F32))
    mixed = (g_four * y_four + g_attn * y_attn).astype(BF16)
    out = jnp.dot(mixed, wo_ref[...], preferred_element_type=F32)
    o_ref[...] = h_ref[...] + _rms(out, gain_ref[...])


def _mixout(yf, ya, g, h, wf, wa, wo, gain):
    seq_ins = [(yf[0], None), (ya[0], None), (g[0], None), (h[0], None)]
    out_seq, out_meta = _token_call(_mixout_body, "mixout", seq_ins, [yf[1], ya[1], g[1], h[1]],
                                    [wf, wa, wo, gain], (D_MODEL,), F32)
    return out_seq[0], out_meta[0]


def _ffn_body(in_refs, const_refs, out_refs, scratch):
    h_ref, = in_refs
    gain_pre_ref, wg_ref, wu_ref, wd_ref, gain_post_ref = const_refs
    o_ref, = out_refs
    act_ref, = scratch
    rows = h_ref.shape[0]
    h = h_ref[...]
    u = _rms(h, gain_pre_ref[...]).astype(BF16)
    for lo, hi in FF_CHUNKS:
        gate = jnp.dot(u, wg_ref[:, lo:hi], preferred_element_type=F32)
        up = jnp.dot(u, wu_ref[:, lo:hi], preferred_element_type=F32)
        act_ref[0:rows, lo:hi] = (gate * jax.nn.sigmoid(gate) * up).astype(BF16)
    ff = jnp.dot(act_ref[0:rows, :], wd_ref[...], preferred_element_type=F32)
    o_ref[...] = h + _rms(ff, gain_post_ref[...])


def _ffn(h, gain_pre, wg, wu, wd, gain_post):
    out_seq, out_meta = _token_call(_ffn_body, "ffn", [(h[0], None)], [h[1]],
                                    [gain_pre, wg, wu, wd, gain_post], (D_MODEL,), F32,
                                    scratch_shapes=[pltpu.VMEM((ROW_TILE, D_FF), BF16)])
    return out_seq[0], out_meta[0]


def _rope_tables(positions):
    d = jnp.arange(LANES) % HEAD_DIM
    inv_freq = ROPE_THETA ** (-(2 * (d % (HEAD_DIM // 2))).astype(F32) / HEAD_DIM)
    ang = positions.astype(F32)[:, None] * inv_freq[None, :]
    sign = jnp.where(d < HEAD_DIM // 2, -1.0, 1.0).astype(F32)
    return jnp.cos(ang), jnp.sin(ang) * sign[None, :]


def _pair_heads(w, axis):
    shape = w.shape
    split = shape[:axis] + (N_KV_HEADS, Q_GROUP, HEAD_DIM) + shape[axis + 1:]
    return jnp.swapaxes(w.reshape(split), axis, axis + 1).reshape(shape)


def kernel(x, meta_tokens, w_in, w_fourier_out, w_attn_out, w_o, sink_logits, norm_mix_pre, norm_mix_post,
           norm_ffn_pre, norm_ffn_post, w_ffn_gate, w_ffn_up, w_ffn_down):
    batch, seq, _ = x.shape
    depth = w_in.shape[0]
    assert seq % ROW_TILE == 0 and seq // BLOCK >= 3

    h = (x.reshape(batch * seq, D_MODEL),
         jnp.broadcast_to(meta_tokens[None].astype(x.dtype), (batch, N_META, D_MODEL)).reshape(-1, D_MODEL))

    rope_seq = _rope_tables(N_META + jnp.arange(seq))
    rope_meta = _rope_tables(jnp.arange(batch * N_META) % N_META)
    fourier_tables = [jnp.asarray(t).astype(BF16) for t in _fourier_tables(seq)]

    gain = lambda t, l: t[l].reshape(1, D_MODEL)

    for l in range(depth):
        w_l = jnp.concatenate([_pair_heads(w_in[l, :, :ATTN_WIDTH], 1), w_in[l, :, ATTN_WIDTH:]], axis=1)
        (q, k, v, f, g), (qm, km, vm, fm, gm) = _inproj(h, gain(norm_mix_pre, l), w_l.astype(BF16),
                                                        rope_seq, rope_meta, seq)
        ya = _attention(sink_logits[l], (q, qm), (k, km), (v, vm), seq)
        yf = _fourier((f, fm), fourier_tables, seq)
        h = _mixout(yf, ya, (g, gm), h, w_fourier_out[l].astype(BF16),
                    _pair_heads(w_attn_out[l], 0).astype(BF16), w_o[l].astype(BF16), gain(norm_mix_post, l))
        h = _ffn(h, gain(norm_ffn_pre, l), w_ffn_gate[l].astype(BF16), w_ffn_up[l].astype(BF16),
                 w_ffn_down[l].astype(BF16), gain(norm_ffn_post, l))

    return h[0].reshape(batch, seq, D_MODEL)
```

```python
import functools

import jax
import jax.numpy as jnp
import numpy as np
from jax import lax
from jax.experimental import pallas as pl
from jax.experimental.pallas import tpu as pltpu

D_MODEL = 1024
N_META = 16
HEAD_DIM = 64
N_HEADS = 8
N_KV_HEADS = 2
Q_GROUP = N_HEADS // N_KV_HEADS
ATTN_WIDTH = N_HEADS * HEAD_DIM
KV_WIDTH = N_KV_HEADS * HEAD_DIM
FOURIER_WIDTH = D_MODEL - ATTN_WIDTH
FOURIER_GROUP = 128
N_FOURIER_GROUPS = FOURIER_WIDTH // FOURIER_GROUP
GATE_WIDTH = 2 * D_MODEL
IN_WIDTH = ATTN_WIDTH + 2 * KV_WIDTH + FOURIER_WIDTH + GATE_WIDTH
PROJ_WIDTHS = (ATTN_WIDTH, KV_WIDTH, KV_WIDTH, FOURIER_WIDTH)
WINDOW = 128
BLOCK = 128
ROPE_THETA = 10000.0
D_FF = 2816
EPS = 1e-6

LANES = 128
VMEM_LIMIT = 56 * 1024 * 1024
ROW_TILE = 1024
FF_CHUNKS = ((0, 1024), (1024, 2048), (2048, D_FF))

BF16 = jnp.bfloat16
F32 = jnp.float32


def _const_spec(shape):
    return pl.BlockSpec(shape, lambda *_: (0,) * len(shape), pipeline_mode=pl.Buffered(1))


def _params():
    return pltpu.CompilerParams(dimension_semantics=("arbitrary",), vmem_limit_bytes=VMEM_LIMIT)


def _rms(x, gain):
    return x * lax.rsqrt(jnp.mean(x * x, axis=-1, keepdims=True) + EPS) * gain


def _token_call(body, name, seq_ins, meta_ins, consts, out_widths, out_dtype, scratch_shapes=()):
    n_seq = seq_ins[0][0].shape[0]
    n_meta = meta_ins[0].shape[0]
    n_tiles = n_seq // ROW_TILE
    n_in, n_const, n_out = len(seq_ins), len(consts), len(out_widths)

    def kernel(*refs):
        seq_refs, refs = refs[:n_in], refs[n_in:]
        meta_refs, refs = refs[:n_in], refs[n_in:]
        const_refs, refs = refs[:n_const], refs[n_const:]
        out_seq, refs = refs[:n_out], refs[n_out:]
        out_meta, scratch = refs[:n_out], refs[n_out:]
        step = pl.program_id(0)

        @pl.when(step < n_tiles)
        def _():
            body(seq_refs, const_refs, out_seq, scratch)

        @pl.when(step == n_tiles)
        def _():
            body(meta_refs, const_refs, out_meta, scratch)

    last = n_tiles - 1
    seq_specs = []
    for arr, index_fn in seq_ins:
        index_fn = index_fn or (lambda i: i)
        seq_specs.append(pl.BlockSpec((ROW_TILE, arr.shape[1]),
                                      functools.partial(lambda fn, i: (fn(jnp.minimum(i, last)), 0), index_fn)))
    meta_specs = [pl.BlockSpec(arr.shape, lambda i: (0, 0)) for arr in meta_ins]
    const_specs = [_const_spec(c.shape) for c in consts]
    out_shape = ([jax.ShapeDtypeStruct((n_seq, w), out_dtype) for w in out_widths]
                 + [jax.ShapeDtypeStruct((n_meta, w), out_dtype) for w in out_widths])
    out_specs = ([pl.BlockSpec((ROW_TILE, w), lambda i: (jnp.minimum(i, last), 0)) for w in out_widths]
                 + [pl.BlockSpec((n_meta, w), lambda i: (0, 0)) for w in out_widths])
    outs = pl.pallas_call(
        kernel,
        out_shape=out_shape,
        grid=(n_tiles + 1,),
        in_specs=seq_specs + meta_specs + const_specs,
        out_specs=out_specs,
        scratch_shapes=list(scratch_shapes),
        compiler_params=_params(),
        name=name,
    )(*[a for a, _ in seq_ins], *meta_ins, *consts)
    return outs[:n_out], outs[n_out:]


def _inproj_body(in_refs, const_refs, out_refs, scratch):
    h_ref, cos_ref, sin_ref = in_refs
    gain_ref, w_ref = const_refs
    q_ref, k_ref, v_ref, f_ref = out_refs
    u = _rms(h_ref[...], gain_ref[...]).astype(BF16)
    cos = cos_ref[...]
    sin = sin_ref[...]
    first_half = (lax.broadcasted_iota(jnp.int32, (1, LANES), 1) % HEAD_DIM) < HEAD_DIM // 2

    def rope(y):
        rot = jnp.where(first_half, pltpu.roll(y, LANES - HEAD_DIM // 2, 1),
                        pltpu.roll(y, HEAD_DIM // 2, 1))
        return y * cos + rot * sin

    def proj(lo, hi):
        return jnp.dot(u, w_ref[:, lo:hi], preferred_element_type=F32)

    q = proj(0, ATTN_WIDTH)
    scale = HEAD_DIM ** -0.5
    for c in range(ATTN_WIDTH // LANES):
        q_ref[:, c * LANES:(c + 1) * LANES] = (rope(q[:, c * LANES:(c + 1) * LANES]) * scale).astype(BF16)
    kv = proj(ATTN_WIDTH, ATTN_WIDTH + 2 * KV_WIDTH)
    k_ref[...] = rope(kv[:, :KV_WIDTH]).astype(BF16)
    v_ref[...] = kv[:, KV_WIDTH:].astype(BF16)
    f0 = ATTN_WIDTH + 2 * KV_WIDTH
    f_ref[...] = proj(f0, f0 + FOURIER_WIDTH).astype(BF16)


def _inproj(h, gain, w, rope_seq, rope_meta, seq):
    tiles_per_seq = seq // ROW_TILE
    table_index = lambda i: i % tiles_per_seq
    seq_ins = [(h[0], None), (rope_seq[0], table_index), (rope_seq[1], table_index)]
    return _token_call(_inproj_body, "inproj", seq_ins, [h[1], rope_meta[0], rope_meta[1]], [gain, w],
                       PROJ_WIDTHS, BF16)


MASKED = -1e30


def _attend(q_rows, k_all, v_ones, bias, sink_ref):
    r = q_rows.shape[0]
    low_lanes = lax.broadcasted_iota(jnp.int32, (1, LANES), 1) < HEAD_DIM
    top_rows = lax.broadcasted_iota(jnp.int32, (2 * r, 1), 0) < r
    bias2 = jnp.concatenate([bias, bias], axis=0)
    zero = jnp.zeros((), BF16)
    outs = []
    for j in range(Q_GROUP):
        qj = q_rows[:, j * LANES:(j + 1) * LANES]
        qm = jnp.concatenate([jnp.where(low_lanes, qj, zero), jnp.where(low_lanes, zero, qj)], axis=0)
        s = lax.dot_general(qm, k_all, (((1,), (1,)), ((), ())), preferred_element_type=F32) + bias2
        sink = jnp.where(top_rows, sink_ref[j], sink_ref[Q_GROUP + j])
        m = jnp.maximum(jnp.max(s, axis=-1, keepdims=True), sink)
        p = jnp.exp((s - m).astype(BF16))
        o = jnp.dot(p, v_ones, preferred_element_type=F32)
        o = o[:, :LANES] / (o[:, LANES:] + jnp.exp(sink - m))
        outs.append(jnp.where(low_lanes, o[:r], o[r:]))
    return jnp.concatenate(outs, axis=1)


def _attn_kernel(sink_ref, q_ref, k_ref, v_ref, qm_ref, km_ref, vm_ref, bias_ref, bias_meta_ref, o_ref, om_ref):
    seq = q_ref.shape[0]
    n_blocks = seq // BLOCK
    band = 3 * BLOCK

    def with_ones(v):
        return jnp.concatenate([v, jnp.ones_like(v)], axis=1)

    n_first = bias_meta_ref.shape[1] - N_META
    k_first = jnp.concatenate([km_ref[...], k_ref[0:n_first, :]], axis=0)
    v_first = jnp.concatenate([vm_ref[...], v_ref[0:n_first, :]], axis=0)
    om_ref[...] = _attend(qm_ref[...], k_first, with_ones(v_first), bias_meta_ref[...], sink_ref).astype(BF16)

    k_head = k_first[0:BLOCK]
    v_head = v_first[0:BLOCK]

    def body(i, carry):
        q_start = pl.multiple_of(BLOCK * i, BLOCK)
        kb = jnp.clip(i - 1, 0, n_blocks - 3)
        k_start = pl.multiple_of(BLOCK * kb, BLOCK)
        kind = jnp.where(i == 0, 0, jnp.where(i == n_blocks - 1, 2, 1))
        k_all = jnp.concatenate([k_head, k_ref[pl.ds(k_start, band), :]], axis=0)
        v_all = jnp.concatenate([v_head, v_ref[pl.ds(k_start, band), :]], axis=0)
        o = _attend(q_ref[pl.ds(q_start, BLOCK), :], k_all, with_ones(v_all), bias_ref[kind], sink_ref)
        o_ref[pl.ds(q_start, BLOCK), :] = o.astype(BF16)
        return carry

    lax.fori_loop(0, n_blocks, body, 0, unroll=8)


def _attention_bias():
    row = np.arange(BLOCK)[:, None]
    col = np.arange(BLOCK + 3 * BLOCK)[None, :]
    kinds = []
    for first_band_block in (0, -1, -2):
        rel = (col - BLOCK) + BLOCK * first_band_block - row
        kinds.append((col < N_META) | ((col >= BLOCK) & (np.abs(rel) <= WINDOW)))
    p_idx = np.arange(N_META)[:, None]
    col = np.arange(2 * BLOCK)[None, :]
    meta = (col < N_META) | (col - p_idx <= WINDOW)
    to_bias = lambda visible: np.where(visible, 0.0, MASKED).astype(np.float32)
    return to_bias(np.stack(kinds)), to_bias(meta)


def _attention(sink, q, k, v, seq):
    n_seq, n_meta = q[0].shape[0], q[1].shape[0]
    bias, bias_meta = _attention_bias()
    per_seq = lambda width: pl.BlockSpec((seq, width), lambda b: (b, 0))
    per_meta = lambda width: pl.BlockSpec((N_META, width), lambda b: (b, 0))
    widths = (ATTN_WIDTH, KV_WIDTH, KV_WIDTH)
    return pl.pallas_call(
        _attn_kernel,
        out_shape=[jax.ShapeDtypeStruct((n_seq, ATTN_WIDTH), BF16), jax.ShapeDtypeStruct((n_meta, ATTN_WIDTH), BF16)],
        grid=(n_seq // seq,),
        in_specs=([pl.BlockSpec(memory_space=pltpu.SMEM)] + [per_seq(w) for w in widths]
                  + [per_meta(w) for w in widths] + [_const_spec(bias.shape), _const_spec(bias_meta.shape)]),
        out_specs=[per_seq(ATTN_WIDTH), per_meta(ATTN_WIDTH)],
        compiler_params=_params(),
        name="attn",
    )(sink, q[0], k[0], v[0], q[1], k[1], v[1], bias, bias_meta)


XCH = 256
XCH_WIN = XCH + N_META


def _fourier_kernel(f_ref, fm_ref, xch_ref, xch_meta_ref, chan_c_ref, chan_s_ref, cos_ref, sin_ref,
                    y_ref, ym_ref, ex_ref, ox_ref, ec_ref, os_ref, t_ref):
    seq = f_ref.shape[0]
    half = seq // 2
    centre = half - N_META // 2
    f32_dot = functools.partial(jnp.dot, preferred_element_type=F32)

    fm = fm_ref[...].astype(F32)
    gm = f32_dot(xch_meta_ref[...], f_ref[seq - N_META:seq, :])
    ex_ref[0:N_META, :] = (fm + gm).astype(BF16)
    ox_ref[0:N_META, :] = (fm - gm).astype(BF16)
    for c in range(half // XCH):
        w = seq - XCH_WIN - XCH * c
        g = f32_dot(xch_ref[...], f_ref[w:w + XCH_WIN, :])
        fs = f_ref[XCH * c:XCH * (c + 1), :].astype(F32)
        if XCH * (c + 1) > centre:
            paired = XCH * c + lax.broadcasted_iota(jnp.int32, (XCH, 1), 0) < centre
            e = fs + jnp.where(paired, g, 0.0)
            o = jnp.where(paired, fs - g, 0.0)
        else:
            e, o = fs + g, fs - g
        rows = slice(N_META + XCH * c, N_META + XCH * (c + 1))
        ex_ref[rows, :] = e.astype(BF16)
        ox_ref[rows, :] = o.astype(BF16)

    for p in range(FOURIER_WIDTH // (2 * FOURIER_GROUP)):
        lanes = slice(2 * FOURIER_GROUP * p, 2 * FOURIER_GROUP * (p + 1))
        ec_ref[:, lanes] = f32_dot(ex_ref[:, lanes], chan_c_ref[...]).astype(BF16)
        os_ref[:, lanes] = f32_dot(ox_ref[:, lanes], chan_s_ref[...]).astype(BF16)

    a = f32_dot(cos_ref[...], ec_ref[...])
    b = f32_dot(sin_ref[...], os_ref[...])
    u = a - b
    t_ref[...] = (a + b).astype(BF16)
    ym_ref[...] = u[0:N_META].astype(BF16)
    y_ref[0:half - N_META, :] = u[N_META:half].astype(BF16)
    mirrored = f32_dot(xch_meta_ref[...], t_ref[half:half + N_META, :])
    low = lax.broadcasted_iota(jnp.int32, (N_META, 1), 0) < N_META // 2
    y_ref[half - N_META:half, :] = jnp.where(low, u[half:half + N_META], mirrored).astype(BF16)
    for c in range(half // XCH):
        start = half + XCH * c
        w = seq - start - XCH
        y_ref[start:start + XCH, :] = f32_dot(xch_ref[...], t_ref[w:w + XCH_WIN, :]).astype(BF16)


def _fourier_tables(seq):
    n = N_META + seq
    h = n // 2
    rows = N_META + seq // 2
    idx = np.arange(rows, dtype=np.int64)
    ang = ((idx[:, None] * idx[None, :]) % n) * (2.0 * np.pi / n)
    inside = idx <= h
    strict = (idx >= 1) & (idx < h)
    cos_t = np.where(inside[:, None] & inside[None, :], np.cos(ang), 0.0) * n ** -0.5
    sin_t = np.where(strict[:, None] & strict[None, :], np.sin(ang), 0.0) * n ** -0.5
    d = np.arange(FOURIER_GROUP, dtype=np.int64)
    ang = ((d[:, None] * d[None, :]) % FOURIER_GROUP) * (2.0 * np.pi / FOURIER_GROUP)
    pair = np.eye(2)
    chan_c = np.kron(pair, np.cos(ang)) * FOURIER_GROUP ** -0.5
    chan_s = np.kron(pair, np.sin(ang)) * FOURIER_GROUP ** -0.5
    xch = np.zeros((XCH, XCH_WIN))
    xch[np.arange(XCH), XCH - np.arange(XCH)] = 1.0
    xch_meta = np.zeros((N_META, N_META))
    xch_meta[np.arange(1, N_META), N_META - np.arange(1, N_META)] = 1.0
    return [t.astype(np.float32) for t in (xch, xch_meta, chan_c, chan_s, cos_t, sin_t)]


def _fourier(f, tables, seq):
    n_seq, n_meta = f[0].shape[0], f[1].shape[0]
    assert seq % (2 * XCH) == 0
    half_rows = N_META + seq // 2
    per_seq = pl.BlockSpec((seq, FOURIER_WIDTH), lambda b: (b, 0))
    per_meta = pl.BlockSpec((N_META, FOURIER_WIDTH), lambda b: (b, 0))
    return pl.pallas_call(
        _fourier_kernel,
        out_shape=[jax.ShapeDtypeStruct((n_seq, FOURIER_WIDTH), BF16),
                   jax.ShapeDtypeStruct((n_meta, FOURIER_WIDTH), BF16)],
        grid=(n_seq // seq,),
        in_specs=[per_seq, per_meta] + [_const_spec(t.shape) for t in tables],
        out_specs=[per_seq, per_meta],
        scratch_shapes=[pltpu.VMEM((half_rows, FOURIER_WIDTH), BF16)] * 5,
        compiler_params=_params(),
        name="fourier",
    )(f[0], f[1], *tables)


def _mixout_body(in_refs, const_refs, out_refs, scratch):
    yf_ref, ya_ref, h_ref = in_refs
    gain_pre_ref, wg_ref, wf_ref, wa_ref, wo_ref, gain_ref = const_refs
    o_ref, = out_refs
    h = h_ref[...]
    u = _rms(h, gain_pre_ref[...]).astype(BF16)
    g_four = jax.nn.sigmoid(jnp.dot(u, wg_ref[:, :D_MODEL], preferred_element_type=F32))
    mixed = g_four * jnp.dot(yf_ref[...], wf_ref[...], preferred_element_type=F32)
    g_attn = jax.nn.sigmoid(jnp.dot(u, wg_ref[:, D_MODEL:], preferred_element_type=F32))
    mixed = (mixed + g_attn * jnp.dot(ya_ref[...], wa_ref[...], preferred_element_type=F32)).astype(BF16)
    out = jnp.dot(mixed, wo_ref[...], preferred_element_type=F32)
    o_ref[...] = h + _rms(out, gain_ref[...])


def _mixout(yf, ya, h, gain_pre, wg, wf, wa, wo, gain):
    seq_ins = [(yf[0], None), (ya[0], None), (h[0], None)]
    out_seq, out_meta = _token_call(_mixout_body, "mixout", seq_ins, [yf[1], ya[1], h[1]],
                                    [gain_pre, wg, wf, wa, wo, gain], (D_MODEL,), F32)
    return out_seq[0], out_meta[0]


def _ffn_body(in_refs, const_refs, out_refs, scratch):
    h_ref, = in_refs
    gain_pre_ref, wg_ref, wu_ref, wd_ref, gain_post_ref = const_refs
    o_ref, = out_refs
    act_ref, = scratch
    rows = h_ref.shape[0]
    h = h_ref[...]
    u = _rms(h, gain_pre_ref[...]).astype(BF16)
    for lo, hi in FF_CHUNKS:
        gate = jnp.dot(u, wg_ref[:, lo:hi], preferred_element_type=F32)
        up = jnp.dot(u, wu_ref[:, lo:hi], preferred_element_type=F32)
        act_ref[0:rows, lo:hi] = (gate * jax.nn.sigmoid(gate) * up).astype(BF16)
    ff = jnp.dot(act_ref[0:rows, :], wd_ref[...], preferred_element_type=F32)
    o_ref[...] = h + _rms(ff, gain_post_ref[...])


def _ffn(h, gain_pre, wg, wu, wd, gain_post):
    out_seq, out_meta = _token_call(_ffn_body, "ffn", [(h[0], None)], [h[1]],
                                    [gain_pre, wg, wu, wd, gain_post], (D_MODEL,), F32,
                                    scratch_shapes=[pltpu.VMEM((ROW_TILE, D_FF), BF16)])
    return out_seq[0], out_meta[0]


def _rope_tables(positions):
    d = jnp.arange(LANES) % HEAD_DIM
    inv_freq = ROPE_THETA ** (-(2 * (d % (HEAD_DIM // 2))).astype(F32) / HEAD_DIM)
    ang = positions.astype(F32)[:, None] * inv_freq[None, :]
    sign = jnp.where(d < HEAD_DIM // 2, -1.0, 1.0).astype(F32)
    return jnp.cos(ang), jnp.sin(ang) * sign[None, :]


def _pair_heads(w, axis):
    shape = w.shape
    split = shape[:axis] + (N_KV_HEADS, Q_GROUP, HEAD_DIM) + shape[axis + 1:]
    return jnp.swapaxes(w.reshape(split), axis, axis + 1).reshape(shape)


def kernel(x, meta_tokens, w_in, w_fourier_out, w_attn_out, w_o, sink_logits, norm_mix_pre, norm_mix_post,
           norm_ffn_pre, norm_ffn_post, w_ffn_gate, w_ffn_up, w_ffn_down):
    batch, seq, _ = x.shape
    depth = w_in.shape[0]
    assert seq % ROW_TILE == 0 and seq // BLOCK >= 3

    h = (x.reshape(batch * seq, D_MODEL),
         jnp.broadcast_to(meta_tokens[None].astype(x.dtype), (batch, N_META, D_MODEL)).reshape(-1, D_MODEL))

    rope_seq = _rope_tables(N_META + jnp.arange(seq))
    rope_meta = _rope_tables(jnp.arange(batch * N_META) % N_META)
    fourier_tables = [jnp.asarray(t).astype(BF16) for t in _fourier_tables(seq)]

    gain = lambda t, l: t[l].reshape(1, D_MODEL)

    for l in range(depth):
        gate0 = IN_WIDTH - GATE_WIDTH
        w_l = jnp.concatenate([_pair_heads(w_in[l, :, :ATTN_WIDTH], 1), w_in[l, :, ATTN_WIDTH:gate0]], axis=1)
        (q, k, v, f), (qm, km, vm, fm) = _inproj(h, gain(norm_mix_pre, l), w_l.astype(BF16),
                                                 rope_seq, rope_meta, seq)
        ya = _attention(sink_logits[l], (q, qm), (k, km), (v, vm), seq)
        yf = _fourier((f, fm), fourier_tables, seq)
        h = _mixout(yf, ya, h, gain(norm_mix_pre, l), w_in[l, :, gate0:].astype(BF16),
                    w_fourier_out[l].astype(BF16), _pair_heads(w_attn_out[l], 0).astype(BF16),
                    w_o[l].astype(BF16), gain(norm_mix_post, l))
        h = _ffn(h, gain(norm_ffn_pre, l), w_ffn_gate[l].astype(BF16), w_ffn_up[l].astype(BF16),
                 w_ffn_down[l].astype(BF16), gain(norm_ffn_post, l))

    return h[0].reshape(batch, seq, D_MODEL)
```

```python
import functools

import jax
import jax.numpy as jnp
import numpy as np
from jax import lax
from jax.experimental import pallas as pl
from jax.experimental.pallas import tpu as pltpu

D_MODEL = 1024
N_META = 16
HEAD_DIM = 64
N_HEADS = 8
N_KV_HEADS = 2
Q_GROUP = N_HEADS // N_KV_HEADS
ATTN_WIDTH = N_HEADS * HEAD_DIM
KV_WIDTH = N_KV_HEADS * HEAD_DIM
FOURIER_WIDTH = D_MODEL - ATTN_WIDTH
FOURIER_GROUP = 128
N_FOURIER_GROUPS = FOURIER_WIDTH // FOURIER_GROUP
GATE_WIDTH = 2 * D_MODEL
IN_WIDTH = ATTN_WIDTH + 2 * KV_WIDTH + FOURIER_WIDTH + GATE_WIDTH
PROJ_WIDTHS = (ATTN_WIDTH, KV_WIDTH, KV_WIDTH, FOURIER_WIDTH)
WINDOW = 128
BLOCK = 128
ROPE_THETA = 10000.0
D_FF = 2816
EPS = 1e-6

LANES = 128
SUBLANES = 8
VMEM_LIMIT = 56 * 1024 * 1024
ROW_TILE = 1024
FF_CHUNKS = ((0, 1024), (1024, 2048), (2048, D_FF))

BF16 = jnp.bfloat16
F32 = jnp.float32


def _const_spec(shape):
    return pl.BlockSpec(shape, lambda *_: (0,) * len(shape), pipeline_mode=pl.Buffered(1))


def _params():
    return pltpu.CompilerParams(dimension_semantics=("arbitrary",), vmem_limit_bytes=VMEM_LIMIT)


def _rms(x, gain):
    return x * lax.rsqrt(jnp.mean(x * x, axis=-1, keepdims=True) + EPS) * gain


def _cast_specs(stacked, layer, n_steps, col_start=0):
    _, rows, cols = stacked.shape
    assert rows % (n_steps * 2 * SUBLANES) == 0 and col_start % LANES == 0
    block_rows = rows // n_steps
    return (pl.BlockSpec((None, block_rows, cols), lambda b: (layer, b, 0)),
            pl.BlockSpec((block_rows, cols - col_start), lambda b: (b, 0)),
            jax.ShapeDtypeStruct((rows, cols - col_start), BF16))


def _cast_blocks(src_refs, dst_refs):
    for src, dst in zip(src_refs, dst_refs):
        dst[...] = src[:, src.shape[1] - dst.shape[1]:].astype(BF16)


def _token_call(body, name, seq_ins, meta_ins, consts, out_widths, out_dtype, scratch_shapes=()):
    n_seq = seq_ins[0][0].shape[0]
    n_meta = meta_ins[0].shape[0]
    n_tiles = n_seq // ROW_TILE
    n_in, n_const, n_out = len(seq_ins), len(consts), len(out_widths)

    def kernel(*refs):
        seq_refs, refs = refs[:n_in], refs[n_in:]
        meta_refs, refs = refs[:n_in], refs[n_in:]
        const_refs, refs = refs[:n_const], refs[n_const:]
        out_seq, refs = refs[:n_out], refs[n_out:]
        out_meta, scratch = refs[:n_out], refs[n_out:]
        step = pl.program_id(0)

        @pl.when(step < n_tiles)
        def _():
            body(seq_refs, const_refs, out_seq, scratch)

        @pl.when(step == n_tiles)
        def _():
            body(meta_refs, const_refs, out_meta, scratch)

    last = n_tiles - 1
    seq_specs = []
    for arr, index_fn in seq_ins:
        index_fn = index_fn or (lambda i: i)
        seq_specs.append(pl.BlockSpec((ROW_TILE, arr.shape[1]),
                                      functools.partial(lambda fn, i: (fn(jnp.minimum(i, last)), 0), index_fn)))
    meta_specs = [pl.BlockSpec(arr.shape, lambda i: (0, 0)) for arr in meta_ins]
    const_specs = [_const_spec(c.shape) for c in consts]
    out_shape = ([jax.ShapeDtypeStruct((n_seq, w), out_dtype) for w in out_widths]
                 + [jax.ShapeDtypeStruct((n_meta, w), out_dtype) for w in out_widths])
    out_specs = ([pl.BlockSpec((ROW_TILE, w), lambda i: (jnp.minimum(i, last), 0)) for w in out_widths]
                 + [pl.BlockSpec((n_meta, w), lambda i: (0, 0)) for w in out_widths])
    outs = pl.pallas_call(
        kernel,
        out_shape=out_shape,
        grid=(n_tiles + 1,),
        in_specs=seq_specs + meta_specs + const_specs,
        out_specs=out_specs,
        scratch_shapes=list(scratch_shapes),
        compiler_params=_params(),
        name=name,
    )(*[a for a, _ in seq_ins], *meta_ins, *consts)
    return outs[:n_out], outs[n_out:]


def _inproj_body(in_refs, const_refs, out_refs, scratch):
    h_ref, cos_ref, sin_ref = in_refs
    gain_ref, w_ref = const_refs
    q_ref, k_ref, v_ref, f_ref = out_refs
    u = _rms(h_ref[...], gain_ref[...]).astype(BF16)
    cos = cos_ref[...]
    sin = sin_ref[...]
    first_half = (lax.broadcasted_iota(jnp.int32, (1, LANES), 1) % HEAD_DIM) < HEAD_DIM // 2

    def rope(y):
        rot = jnp.where(first_half, pltpu.roll(y, LANES - HEAD_DIM // 2, 1),
                        pltpu.roll(y, HEAD_DIM // 2, 1))
        return y * cos + rot * sin

    def proj(lo, hi):
        return jnp.dot(u, w_ref[:, lo:hi], preferred_element_type=F32)

    q = proj(0, ATTN_WIDTH)
    scale = HEAD_DIM ** -0.5
    for c in range(ATTN_WIDTH // LANES):
        q_ref[:, c * LANES:(c + 1) * LANES] = (rope(q[:, c * LANES:(c + 1) * LANES]) * scale).astype(BF16)
    kv = proj(ATTN_WIDTH, ATTN_WIDTH + 2 * KV_WIDTH)
    k_ref[...] = rope(kv[:, :KV_WIDTH]).astype(BF16)
    v_ref[...] = kv[:, KV_WIDTH:].astype(BF16)
    f0 = ATTN_WIDTH + 2 * KV_WIDTH
    f_ref[...] = proj(f0, f0 + FOURIER_WIDTH).astype(BF16)


def _inproj(h, gain, w, rope_seq, rope_meta, seq):
    tiles_per_seq = seq // ROW_TILE
    table_index = lambda i: i % tiles_per_seq
    seq_ins = [(h[0], None), (rope_seq[0], table_index), (rope_seq[1], table_index)]
    return _token_call(_inproj_body, "inproj", seq_ins, [h[1], rope_meta[0], rope_meta[1]], [gain, w],
                       PROJ_WIDTHS, BF16)


MASKED = -1e30


def _attend(q_rows, k_all, v_ones, bias, sink_ref):
    r = q_rows.shape[0]
    low_lanes = lax.broadcasted_iota(jnp.int32, (1, LANES), 1) < HEAD_DIM
    top_rows = lax.broadcasted_iota(jnp.int32, (2 * r, 1), 0) < r
    bias2 = jnp.concatenate([bias, bias], axis=0)
    zero = jnp.zeros((), BF16)
    outs = []
    for j in range(Q_GROUP):
        qj = q_rows[:, j * LANES:(j + 1) * LANES]
        qm = jnp.concatenate([jnp.where(low_lanes, qj, zero), jnp.where(low_lanes, zero, qj)], axis=0)
        s = lax.dot_general(qm, k_all, (((1,), (1,)), ((), ())), preferred_element_type=F32) + bias2
        sink = jnp.where(top_rows, sink_ref[j], sink_ref[Q_GROUP + j])
        m = jnp.maximum(jnp.max(s, axis=-1, keepdims=True), sink)
        p = jnp.exp((s - m).astype(BF16))
        o = jnp.dot(p, v_ones, preferred_element_type=F32)
        o = o[:, :LANES] / (o[:, LANES:] + jnp.exp(sink - m))
        outs.append(jnp.where(low_lanes, o[:r], o[r:]))
    return jnp.concatenate(outs, axis=1)


def _attn_kernel(sink_ref, q_ref, k_ref, v_ref, qm_ref, km_ref, vm_ref, bias_ref, bias_meta_ref,
                 wa_ref, wb_ref, wc_ref, o_ref, om_ref, wa_out, wb_out, wc_out):
    _cast_blocks((wa_ref, wb_ref, wc_ref), (wa_out, wb_out, wc_out))
    seq = q_ref.shape[0]
    n_blocks = seq // BLOCK
    band = 3 * BLOCK

    def with_ones(v):
        return jnp.concatenate([v, jnp.ones_like(v)], axis=1)

    n_first = bias_meta_ref.shape[1] - N_META
    k_first = jnp.concatenate([km_ref[...], k_ref[0:n_first, :]], axis=0)
    v_first = jnp.concatenate([vm_ref[...], v_ref[0:n_first, :]], axis=0)
    om_ref[...] = _attend(qm_ref[...], k_first, with_ones(v_first), bias_meta_ref[...], sink_ref).astype(BF16)

    k_head = k_first[0:BLOCK]
    v_head = v_first[0:BLOCK]

    def body(i, carry):
        q_start = pl.multiple_of(BLOCK * i, BLOCK)
        kb = jnp.clip(i - 1, 0, n_blocks - 3)
        k_start = pl.multiple_of(BLOCK * kb, BLOCK)
        kind = jnp.where(i == 0, 0, jnp.where(i == n_blocks - 1, 2, 1))
        k_all = jnp.concatenate([k_head, k_ref[pl.ds(k_start, band), :]], axis=0)
        v_all = jnp.concatenate([v_head, v_ref[pl.ds(k_start, band), :]], axis=0)
        o = _attend(q_ref[pl.ds(q_start, BLOCK), :], k_all, with_ones(v_all), bias_ref[kind], sink_ref)
        o_ref[pl.ds(q_start, BLOCK), :] = o.astype(BF16)
        return carry

    lax.fori_loop(0, n_blocks, body, 0, unroll=8)


def _attention_bias():
    row = np.arange(BLOCK)[:, None]
    col = np.arange(BLOCK + 3 * BLOCK)[None, :]
    kinds = []
    for first_band_block in (0, -1, -2):
        rel = (col - BLOCK) + BLOCK * first_band_block - row
        kinds.append((col < N_META) | ((col >= BLOCK) & (np.abs(rel) <= WINDOW)))
    p_idx = np.arange(N_META)[:, None]
    col = np.arange(2 * BLOCK)[None, :]
    meta = (col < N_META) | (col - p_idx <= WINDOW)
    to_bias = lambda visible: np.where(visible, 0.0, MASKED).astype(np.float32)
    return to_bias(np.stack(kinds)), to_bias(meta)


def _attention(sink, q, k, v, seq, layer, cast_jobs):
    n_seq, n_meta = q[0].shape[0], q[1].shape[0]
    n_steps = n_seq // seq
    bias, bias_meta = _attention_bias()
    per_seq = lambda width: pl.BlockSpec((seq, width), lambda b: (b, 0))
    per_meta = lambda width: pl.BlockSpec((N_META, width), lambda b: (b, 0))
    widths = (ATTN_WIDTH, KV_WIDTH, KV_WIDTH)
    w_in_specs, w_out_specs, w_out_shapes = zip(*[_cast_specs(w, layer, n_steps, c0) for w, c0 in cast_jobs])
    outs = pl.pallas_call(
        _attn_kernel,
        out_shape=[jax.ShapeDtypeStruct((n_seq, ATTN_WIDTH), BF16),
                   jax.ShapeDtypeStruct((n_meta, ATTN_WIDTH), BF16)] + list(w_out_shapes),
        grid=(n_steps,),
        in_specs=([pl.BlockSpec(memory_space=pltpu.SMEM)] + [per_seq(w) for w in widths]
                  + [per_meta(w) for w in widths] + [_const_spec(bias.shape), _const_spec(bias_meta.shape)]
                  + list(w_in_specs)),
        out_specs=[per_seq(ATTN_WIDTH), per_meta(ATTN_WIDTH)] + list(w_out_specs),
        compiler_params=_params(),
        name="attn",
    )(sink, q[0], k[0], v[0], q[1], k[1], v[1], bias, bias_meta, *[w for w, _ in cast_jobs])
    return (outs[0], outs[1]), outs[2:]


XCH = 256
XCH_WIN = XCH + N_META


def _fourier_kernel(f_ref, fm_ref, xch_ref, xch_meta_ref, chan_c_ref, chan_s_ref, cos_ref, sin_ref,
                    wa_ref, wb_ref, wc_ref, y_ref, ym_ref, wa_out, wb_out, wc_out,
                    ex_ref, ox_ref, ec_ref, os_ref, t_ref):
    _cast_blocks((wa_ref, wb_ref, wc_ref), (wa_out, wb_out, wc_out))

    seq = f_ref.shape[0]
    half = seq // 2
    centre = half - N_META // 2
    f32_dot = functools.partial(jnp.dot, preferred_element_type=F32)

    fm = fm_ref[...].astype(F32)
    gm = f32_dot(xch_meta_ref[...], f_ref[seq - N_META:seq, :])
    ex_ref[0:N_META, :] = (fm + gm).astype(BF16)
    ox_ref[0:N_META, :] = (fm - gm).astype(BF16)
    for c in range(half // XCH):
        w = seq - XCH_WIN - XCH * c
        g = f32_dot(xch_ref[...], f_ref[w:w + XCH_WIN, :])
        fs = f_ref[XCH * c:XCH * (c + 1), :].astype(F32)
        if XCH * (c + 1) > centre:
            paired = XCH * c + lax.broadcasted_iota(jnp.int32, (XCH, 1), 0) < centre
            e = fs + jnp.where(paired, g, 0.0)
            o = jnp.where(paired, fs - g, 0.0)
        else:
            e, o = fs + g, fs - g
        rows = slice(N_META + XCH * c, N_META + XCH * (c + 1))
        ex_ref[rows, :] = e.astype(BF16)
        ox_ref[rows, :] = o.astype(BF16)

    for p in range(FOURIER_WIDTH // (2 * FOURIER_GROUP)):
        lanes = slice(2 * FOURIER_GROUP * p, 2 * FOURIER_GROUP * (p + 1))
        ec_ref[:, lanes] = f32_dot(ex_ref[:, lanes], chan_c_ref[...]).astype(BF16)
        os_ref[:, lanes] = f32_dot(ox_ref[:, lanes], chan_s_ref[...]).astype(BF16)

    a = f32_dot(cos_ref[...], ec_ref[...])
    b = f32_dot(sin_ref[...], os_ref[...])
    u = a - b
    t_ref[...] = (a + b).astype(BF16)
    ym_ref[...] = u[0:N_META].astype(BF16)
    y_ref[0:half - N_META, :] = u[N_META:half].astype(BF16)
    mirrored = f32_dot(xch_meta_ref[...], t_ref[half:half + N_META, :])
    low = lax.broadcasted_iota(jnp.int32, (N_META, 1), 0) < N_META // 2
    y_ref[half - N_META:half, :] = jnp.where(low, u[half:half + N_META], mirrored).astype(BF16)
    for c in range(half // XCH):
        start = half + XCH * c
        w = seq - start - XCH
        y_ref[start:start + XCH, :] = f32_dot(xch_ref[...], t_ref[w:w + XCH_WIN, :]).astype(BF16)


def _fourier_tables(seq):
    n = N_META + seq
    h = n // 2
    rows = N_META + seq // 2
    idx = np.arange(rows, dtype=np.int64)
    ang = ((idx[:, None] * idx[None, :]) % n) * (2.0 * np.pi / n)
    inside = idx <= h
    strict = (idx >= 1) & (idx < h)
    cos_t = np.where(inside[:, None] & inside[None, :], np.cos(ang), 0.0) * n ** -0.5
    sin_t = np.where(strict[:, None] & strict[None, :], np.sin(ang), 0.0) * n ** -0.5
    d = np.arange(FOURIER_GROUP, dtype=np.int64)
    ang = ((d[:, None] * d[None, :]) % FOURIER_GROUP) * (2.0 * np.pi / FOURIER_GROUP)
    pair = np.eye(2)
    chan_c = np.kron(pair, np.cos(ang)) * FOURIER_GROUP ** -0.5
    chan_s = np.kron(pair, np.sin(ang)) * FOURIER_GROUP ** -0.5
    xch = np.zeros((XCH, XCH_WIN))
    xch[np.arange(XCH), XCH - np.arange(XCH)] = 1.0
    xch_meta = np.zeros((N_META, N_META))
    xch_meta[np.arange(1, N_META), N_META - np.arange(1, N_META)] = 1.0
    return [t.astype(np.float32) for t in (xch, xch_meta, chan_c, chan_s, cos_t, sin_t)]


def _fourier(f, tables, seq, layer, stacked_weights):
    n_seq, n_meta = f[0].shape[0], f[1].shape[0]
    n_steps = n_seq // seq
    assert seq % (2 * XCH) == 0
    half_rows = N_META + seq // 2
    per_seq = pl.BlockSpec((seq, FOURIER_WIDTH), lambda b: (b, 0))
    per_meta = pl.BlockSpec((N_META, FOURIER_WIDTH), lambda b: (b, 0))
    w_in_specs, w_out_specs, w_out_shapes = zip(*[_cast_specs(w, layer, n_steps) for w in stacked_weights])
    w_in_specs, w_out_specs, w_out_shapes = list(w_in_specs), list(w_out_specs), list(w_out_shapes)
    outs = pl.pallas_call(
        _fourier_kernel,
        out_shape=[jax.ShapeDtypeStruct((n_seq, FOURIER_WIDTH), BF16),
                   jax.ShapeDtypeStruct((n_meta, FOURIER_WIDTH), BF16)] + w_out_shapes,
        grid=(n_steps,),
        in_specs=[per_seq, per_meta] + [_const_spec(t.shape) for t in tables] + w_in_specs,
        out_specs=[per_seq, per_meta] + w_out_specs,
        scratch_shapes=[pltpu.VMEM((half_rows, FOURIER_WIDTH), BF16)] * 5,
        compiler_params=_params(),
        name="fourier",
    )(f[0], f[1], *tables, *stacked_weights)
    return (outs[0], outs[1]), outs[2:]


def _mixout_body(in_refs, const_refs, out_refs, scratch):
    yf_ref, ya_ref, h_ref = in_refs
    gain_pre_ref, wg_ref, wf_ref, wa_ref, wo_ref, gain_ref = const_refs
    o_ref, = out_refs
    h = h_ref[...]
    u = _rms(h, gain_pre_ref[...]).astype(BF16)
    g_four = jax.nn.sigmoid(jnp.dot(u, wg_ref[:, :D_MODEL], preferred_element_type=F32))
    mixed = g_four * jnp.dot(yf_ref[...], wf_ref[...], preferred_element_type=F32)
    g_attn = jax.nn.sigmoid(jnp.dot(u, wg_ref[:, D_MODEL:], preferred_element_type=F32))
    mixed = (mixed + g_attn * jnp.dot(ya_ref[...], wa_ref[...], preferred_element_type=F32)).astype(BF16)
    out = jnp.dot(mixed, wo_ref[...], preferred_element_type=F32)
    o_ref[...] = h + _rms(out, gain_ref[...])


def _mixout(yf, ya, h, gain_pre, wg, wf, wa, wo, gain):
    seq_ins = [(yf[0], None), (ya[0], None), (h[0], None)]
    out_seq, out_meta = _token_call(_mixout_body, "mixout", seq_ins, [yf[1], ya[1], h[1]],
                                    [gain_pre, wg, wf, wa, wo, gain], (D_MODEL,), F32)
    return out_seq[0], out_meta[0]


def _ffn_body(in_refs, const_refs, out_refs, scratch):
    h_ref, = in_refs
    gain_pre_ref, wg_ref, wu_ref, wd_ref, gain_post_ref = const_refs
    o_ref, = out_refs
    act_ref, = scratch
    rows = h_ref.shape[0]
    h = h_ref[...]
    u = _rms(h, gain_pre_ref[...]).astype(BF16)
    for lo, hi in FF_CHUNKS:
        gate = jnp.dot(u, wg_ref[:, lo:hi], preferred_element_type=F32)
        up = jnp.dot(u, wu_ref[:, lo:hi], preferred_element_type=F32)
        act_ref[0:rows, lo:hi] = (gate * jax.nn.sigmoid(gate) * up).astype(BF16)
    ff = jnp.dot(act_ref[0:rows, :], wd_ref[...], preferred_element_type=F32)
    o_ref[...] = h + _rms(ff, gain_post_ref[...])


def _ffn(h, gain_pre, wg, wu, wd, gain_post):
    out_seq, out_meta = _token_call(_ffn_body, "ffn", [(h[0], None)], [h[1]],
                                    [gain_pre, wg, wu, wd, gain_post], (D_MODEL,), F32,
                                    scratch_shapes=[pltpu.VMEM((ROW_TILE, D_FF), BF16)])
    return out_seq[0], out_meta[0]


def _rope_tables(positions):
    d = jnp.arange(LANES) % HEAD_DIM
    inv_freq = ROPE_THETA ** (-(2 * (d % (HEAD_DIM // 2))).astype(F32) / HEAD_DIM)
    ang = positions.astype(F32)[:, None] * inv_freq[None, :]
    sign = jnp.where(d < HEAD_DIM // 2, -1.0, 1.0).astype(F32)
    return jnp.cos(ang), jnp.sin(ang) * sign[None, :]


def _pair_heads(w, axis):
    shape = w.shape
    split = shape[:axis] + (N_KV_HEADS, Q_GROUP, HEAD_DIM) + shape[axis + 1:]
    return jnp.swapaxes(w.reshape(split), axis, axis + 1).reshape(shape)


def kernel(x, meta_tokens, w_in, w_fourier_out, w_attn_out, w_o, sink_logits, norm_mix_pre, norm_mix_post,
           norm_ffn_pre, norm_ffn_post, w_ffn_gate, w_ffn_up, w_ffn_down):
    batch, seq, _ = x.shape
    depth = w_in.shape[0]
    assert seq % ROW_TILE == 0 and seq // BLOCK >= 3

    h = (x.reshape(batch * seq, D_MODEL),
         jnp.broadcast_to(meta_tokens[None].astype(x.dtype), (batch, N_META, D_MODEL)).reshape(-1, D_MODEL))

    rope_seq = _rope_tables(N_META + jnp.arange(seq))
    rope_meta = _rope_tables(jnp.arange(batch * N_META) % N_META)
    fourier_tables = [jnp.asarray(t).astype(BF16) for t in _fourier_tables(seq)]

    gain = lambda t, l: t[l].reshape(1, D_MODEL)

    for l in range(depth):
        gate0 = IN_WIDTH - GATE_WIDTH
        w_l = jnp.concatenate([_pair_heads(w_in[l, :, :ATTN_WIDTH], 1), w_in[l, :, ATTN_WIDTH:gate0]], axis=1)
        (q, k, v, f), (qm, km, vm, fm) = _inproj(h, gain(norm_mix_pre, l), w_l.astype(BF16),
                                                 rope_seq, rope_meta, seq)
        ya, (w_gate, w_four, w_out) = _attention(sink_logits[l], (q, qm), (k, km), (v, vm), seq, l,
                                                 ((w_in, gate0), (w_fourier_out, 0), (w_o, 0)))
        yf, (wg, wu, wd) = _fourier((f, fm), fourier_tables, seq, l, (w_ffn_gate, w_ffn_up, w_ffn_down))
        h = _mixout(yf, ya, h, gain(norm_mix_pre, l), w_gate, w_four,
                    _pair_heads(w_attn_out[l], 0).astype(BF16), w_out, gain(norm_mix_post, l))
        h = _ffn(h, gain(norm_ffn_pre, l), wg, wu, wd, gain(norm_ffn_post, l))

    return h[0].reshape(batch, seq, D_MODEL)
```

```python
import functools

import jax
import jax.numpy as jnp
import numpy as np
from jax import lax
from jax.experimental import pallas as pl
from jax.experimental.pallas import tpu as pltpu

D_MODEL = 1024
N_META = 16
HEAD_DIM = 64
N_HEADS = 8
N_KV_HEADS = 2
Q_GROUP = N_HEADS // N_KV_HEADS
ATTN_WIDTH = N_HEADS * HEAD_DIM
KV_WIDTH = N_KV_HEADS * HEAD_DIM
FOURIER_WIDTH = D_MODEL - ATTN_WIDTH
FOURIER_GROUP = 128
N_FOURIER_GROUPS = FOURIER_WIDTH // FOURIER_GROUP
GATE_WIDTH = 2 * D_MODEL
IN_WIDTH = ATTN_WIDTH + 2 * KV_WIDTH + FOURIER_WIDTH + GATE_WIDTH
PROJ_WIDTHS = (ATTN_WIDTH, KV_WIDTH, KV_WIDTH, FOURIER_WIDTH)
WINDOW = 128
BLOCK = 128
ROPE_THETA = 10000.0
D_FF = 2816
EPS = 1e-6

LANES = 128
SUBLANES = 8
VMEM_LIMIT = 56 * 1024 * 1024
ROW_TILE = 1024
FF_CHUNKS = ((0, 1536), (1536, D_FF))

BF16 = jnp.bfloat16
F32 = jnp.float32


def _const_spec(shape):
    return pl.BlockSpec(shape, lambda *_: (0,) * len(shape), pipeline_mode=pl.Buffered(1))


def _params():
    return pltpu.CompilerParams(dimension_semantics=("arbitrary",), vmem_limit_bytes=VMEM_LIMIT)


def _rms(x, gain):
    return x * lax.rsqrt(jnp.mean(x * x, axis=-1, keepdims=True) + EPS) * gain


def _cast_specs(stacked, layer, n_steps, col_start=0):
    _, rows, cols = stacked.shape
    assert rows % (n_steps * 2 * SUBLANES) == 0 and col_start % LANES == 0
    block_rows = rows // n_steps
    return (pl.BlockSpec((None, block_rows, cols), lambda b: (layer, b, 0)),
            pl.BlockSpec((block_rows, cols - col_start), lambda b: (b, 0)),
            jax.ShapeDtypeStruct((rows, cols - col_start), BF16))


def _cast_blocks(src_refs, dst_refs):
    for src, dst in zip(src_refs, dst_refs):
        dst[...] = src[:, src.shape[1] - dst.shape[1]:].astype(BF16)


def _token_call(body, name, seq_ins, meta_ins, consts, out_widths, out_dtype, row_tile=ROW_TILE):
    n_seq = seq_ins[0][0].shape[0]
    n_meta = meta_ins[0].shape[0]
    n_tiles = n_seq // row_tile
    n_in, n_const, n_out = len(seq_ins), len(consts), len(out_widths)

    def kernel(*refs):
        seq_refs, refs = refs[:n_in], refs[n_in:]
        meta_refs, refs = refs[:n_in], refs[n_in:]
        const_refs, refs = refs[:n_const], refs[n_const:]
        out_seq, refs = refs[:n_out], refs[n_out:]
        out_meta = refs[:n_out]
        step = pl.program_id(0)

        @pl.when(step < n_tiles)
        def _():
            body(seq_refs, const_refs, out_seq)

        @pl.when(step == n_tiles)
        def _():
            body(meta_refs, const_refs, out_meta)

    last = n_tiles - 1
    seq_specs = []
    for arr, index_fn in seq_ins:
        index_fn = index_fn or (lambda i: i)
        seq_specs.append(pl.BlockSpec((row_tile, arr.shape[1]),
                                      functools.partial(lambda fn, i: (fn(jnp.minimum(i, last)), 0), index_fn)))
    meta_specs = [pl.BlockSpec(arr.shape, lambda i: (0, 0)) for arr in meta_ins]
    const_specs = [_const_spec(c.shape) for c in consts]
    out_shape = ([jax.ShapeDtypeStruct((n_seq, w), out_dtype) for w in out_widths]
                 + [jax.ShapeDtypeStruct((n_meta, w), out_dtype) for w in out_widths])
    out_specs = ([pl.BlockSpec((row_tile, w), lambda i: (jnp.minimum(i, last), 0)) for w in out_widths]
                 + [pl.BlockSpec((n_meta, w), lambda i: (0, 0)) for w in out_widths])
    outs = pl.pallas_call(
        kernel,
        out_shape=out_shape,
        grid=(n_tiles + 1,),
        in_specs=seq_specs + meta_specs + const_specs,
        out_specs=out_specs,
        compiler_params=_params(),
        name=name,
    )(*[a for a, _ in seq_ins], *meta_ins, *consts)
    return outs[:n_out], outs[n_out:]


def _inproj_body(in_refs, const_refs, out_refs):
    h_ref, cos_ref, sin_ref = in_refs
    gain_ref, w_ref = const_refs
    q_ref, k_ref, v_ref, f_ref = out_refs
    u = _rms(h_ref[...], gain_ref[...]).astype(BF16)
    cos = cos_ref[...]
    sin = sin_ref[...]
    first_half = (lax.broadcasted_iota(jnp.int32, (1, LANES), 1) % HEAD_DIM) < HEAD_DIM // 2

    def rope(y):
        rot = jnp.where(first_half, pltpu.roll(y, LANES - HEAD_DIM // 2, 1),
                        pltpu.roll(y, HEAD_DIM // 2, 1))
        return y * cos + rot * sin

    def proj(lo, hi):
        return jnp.dot(u, w_ref[:, lo:hi], preferred_element_type=F32)

    q = proj(0, ATTN_WIDTH)
    scale = HEAD_DIM ** -0.5
    for c in range(ATTN_WIDTH // LANES):
        q_ref[:, c * LANES:(c + 1) * LANES] = (rope(q[:, c * LANES:(c + 1) * LANES]) * scale).astype(BF16)
    kv = proj(ATTN_WIDTH, ATTN_WIDTH + 2 * KV_WIDTH)
    k_ref[...] = rope(kv[:, :KV_WIDTH]).astype(BF16)
    v_ref[...] = kv[:, KV_WIDTH:].astype(BF16)
    f0 = ATTN_WIDTH + 2 * KV_WIDTH
    f_ref[...] = proj(f0, f0 + FOURIER_WIDTH).astype(BF16)


def _inproj(h, gain, w, rope_seq, rope_meta, seq):
    table_index = lambda i: 0
    seq_ins = [(h[0], None), (rope_seq[0], table_index), (rope_seq[1], table_index)]
    return _token_call(_inproj_body, "inproj", seq_ins, [h[1], rope_meta[0], rope_meta[1]], [gain, w],
                       PROJ_WIDTHS, BF16, row_tile=seq)


MASKED = -1e30


def _attend(q_rows, k_all, v_ones, bias, sink_ref):
    r = q_rows.shape[0]
    low_lanes = lax.broadcasted_iota(jnp.int32, (1, LANES), 1) < HEAD_DIM
    top_rows = lax.broadcasted_iota(jnp.int32, (2 * r, 1), 0) < r
    bias2 = jnp.concatenate([bias, bias], axis=0)
    zero = jnp.zeros((), BF16)
    outs = []
    for j in range(Q_GROUP):
        qj = q_rows[:, j * LANES:(j + 1) * LANES]
        qm = jnp.concatenate([jnp.where(low_lanes, qj, zero), jnp.where(low_lanes, zero, qj)], axis=0)
        s = lax.dot_general(qm, k_all, (((1,), (1,)), ((), ())), preferred_element_type=F32) + bias2
        sink = jnp.where(top_rows, sink_ref[j], sink_ref[Q_GROUP + j])
        m = jnp.maximum(jnp.max(s, axis=-1, keepdims=True), sink)
        p = jnp.exp((s - m).astype(BF16))
        o = jnp.dot(p, v_ones, preferred_element_type=F32)
        o = o[:, :LANES] / (o[:, LANES:] + jnp.exp(sink - m))
        outs.append(jnp.where(low_lanes, o[:r], o[r:]))
    return jnp.concatenate(outs, axis=1)


def _attn_kernel(sink_ref, q_ref, k_ref, v_ref, qm_ref, km_ref, vm_ref, bias_ref, bias_meta_ref,
                 wa_ref, wb_ref, wc_ref, o_ref, om_ref, wa_out, wb_out, wc_out):
    _cast_blocks((wa_ref, wb_ref, wc_ref), (wa_out, wb_out, wc_out))
    seq = q_ref.shape[0]
    n_blocks = seq // BLOCK
    band = 3 * BLOCK

    def with_ones(v):
        return jnp.concatenate([v, jnp.ones_like(v)], axis=1)

    n_first = bias_meta_ref.shape[1] - N_META
    k_first = jnp.concatenate([km_ref[...], k_ref[0:n_first, :]], axis=0)
    v_first = jnp.concatenate([vm_ref[...], v_ref[0:n_first, :]], axis=0)
    om_ref[...] = _attend(qm_ref[...], k_first, with_ones(v_first), bias_meta_ref[...], sink_ref).astype(BF16)

    k_head = k_first[0:BLOCK]
    v_head = v_first[0:BLOCK]

    def body(i, carry):
        q_start = pl.multiple_of(BLOCK * i, BLOCK)
        kb = jnp.clip(i - 1, 0, n_blocks - 3)
        k_start = pl.multiple_of(BLOCK * kb, BLOCK)
        kind = jnp.where(i == 0, 0, jnp.where(i == n_blocks - 1, 2, 1))
        k_all = jnp.concatenate([k_head, k_ref[pl.ds(k_start, band), :]], axis=0)
        v_all = jnp.concatenate([v_head, v_ref[pl.ds(k_start, band), :]], axis=0)
        o = _attend(q_ref[pl.ds(q_start, BLOCK), :], k_all, with_ones(v_all), bias_ref[kind], sink_ref)
        o_ref[pl.ds(q_start, BLOCK), :] = o.astype(BF16)
        return carry

    lax.fori_loop(0, n_blocks, body, 0, unroll=16)


def _attention_bias():
    row = np.arange(BLOCK)[:, None]
    col = np.arange(BLOCK + 3 * BLOCK)[None, :]
    kinds = []
    for first_band_block in (0, -1, -2):
        rel = (col - BLOCK) + BLOCK * first_band_block - row
        kinds.append((col < N_META) | ((col >= BLOCK) & (np.abs(rel) <= WINDOW)))
    p_idx = np.arange(N_META)[:, None]
    col = np.arange(2 * BLOCK)[None, :]
    meta = (col < N_META) | (col - p_idx <= WINDOW)
    to_bias = lambda visible: np.where(visible, 0.0, MASKED).astype(np.float32)
    return to_bias(np.stack(kinds)), to_bias(meta)


def _attention(sink, q, k, v, seq, layer, cast_jobs):
    n_seq, n_meta = q[0].shape[0], q[1].shape[0]
    n_steps = n_seq // seq
    bias, bias_meta = _attention_bias()
    per_seq = lambda width: pl.BlockSpec((seq, width), lambda b: (b, 0))
    per_meta = lambda width: pl.BlockSpec((N_META, width), lambda b: (b, 0))
    widths = (ATTN_WIDTH, KV_WIDTH, KV_WIDTH)
    w_in_specs, w_out_specs, w_out_shapes = zip(*[_cast_specs(w, layer, n_steps, c0) for w, c0 in cast_jobs])
    outs = pl.pallas_call(
        _attn_kernel,
        out_shape=[jax.ShapeDtypeStruct((n_seq, ATTN_WIDTH), BF16),
                   jax.ShapeDtypeStruct((n_meta, ATTN_WIDTH), BF16)] + list(w_out_shapes),
        grid=(n_steps,),
        in_specs=([pl.BlockSpec(memory_space=pltpu.SMEM)] + [per_seq(w) for w in widths]
                  + [per_meta(w) for w in widths] + [_const_spec(bias.shape), _const_spec(bias_meta.shape)]
                  + list(w_in_specs)),
        out_specs=[per_seq(ATTN_WIDTH), per_meta(ATTN_WIDTH)] + list(w_out_specs),
        compiler_params=_params(),
        name="attn",
    )(sink, q[0], k[0], v[0], q[1], k[1], v[1], bias, bias_meta, *[w for w, _ in cast_jobs])
    return (outs[0], outs[1]), outs[2:]


XCH = 256
XCH_WIN = XCH + N_META


def _fourier_kernel(f_ref, fm_ref, xch_ref, xch_meta_ref, chan_c_ref, chan_s_ref, cos_ref, sin_ref,
                    wa_ref, wb_ref, wc_ref, y_ref, ym_ref, wa_out, wb_out, wc_out,
                    ex_ref, ox_ref, ec_ref, os_ref, t_ref):
    _cast_blocks((wa_ref, wb_ref, wc_ref), (wa_out, wb_out, wc_out))

    seq = f_ref.shape[0]
    half = seq // 2
    centre = half - N_META // 2
    f32_dot = functools.partial(jnp.dot, preferred_element_type=F32)

    fm = fm_ref[...].astype(F32)
    gm = f32_dot(xch_meta_ref[...], f_ref[seq - N_META:seq, :])
    ex_ref[0:N_META, :] = (fm + gm).astype(BF16)
    ox_ref[0:N_META, :] = (fm - gm).astype(BF16)
    for c in range(half // XCH):
        w = seq - XCH_WIN - XCH * c
        g = f32_dot(xch_ref[...], f_ref[w:w + XCH_WIN, :])
        fs = f_ref[XCH * c:XCH * (c + 1), :].astype(F32)
        if XCH * (c + 1) > centre:
            paired = XCH * c + lax.broadcasted_iota(jnp.int32, (XCH, 1), 0) < centre
            e = fs + jnp.where(paired, g, 0.0)
            o = jnp.where(paired, fs - g, 0.0)
        else:
            e, o = fs + g, fs - g
        rows = slice(N_META + XCH * c, N_META + XCH * (c + 1))
        ex_ref[rows, :] = e.astype(BF16)
        ox_ref[rows, :] = o.astype(BF16)

    for p in range(FOURIER_WIDTH // (2 * FOURIER_GROUP)):
        lanes = slice(2 * FOURIER_GROUP * p, 2 * FOURIER_GROUP * (p + 1))
        ec_ref[:, lanes] = f32_dot(ex_ref[:, lanes], chan_c_ref[...]).astype(BF16)
        os_ref[:, lanes] = f32_dot(ox_ref[:, lanes], chan_s_ref[...]).astype(BF16)

    a = f32_dot(cos_ref[...], ec_ref[...])
    b = f32_dot(sin_ref[...], os_ref[...])
    u = a - b
    t_ref[...] = (a + b).astype(BF16)
    ym_ref[...] = u[0:N_META].astype(BF16)
    y_ref[0:half - N_META, :] = u[N_META:half].astype(BF16)
    mirrored = f32_dot(xch_meta_ref[...], t_ref[half:half + N_META, :])
    low = lax.broadcasted_iota(jnp.int32, (N_META, 1), 0) < N_META // 2
    y_ref[half - N_META:half, :] = jnp.where(low, u[half:half + N_META], mirrored).astype(BF16)
    for c in range(half // XCH):
        start = half + XCH * c
        w = seq - start - XCH
        y_ref[start:start + XCH, :] = f32_dot(xch_ref[...], t_ref[w:w + XCH_WIN, :]).astype(BF16)


def _fourier_tables(seq):
    n = N_META + seq
    h = n // 2
    rows = N_META + seq // 2
    idx = np.arange(rows, dtype=np.int64)
    ang = ((idx[:, None] * idx[None, :]) % n) * (2.0 * np.pi / n)
    inside = idx <= h
    strict = (idx >= 1) & (idx < h)
    cos_t = np.where(inside[:, None] & inside[None, :], np.cos(ang), 0.0) * n ** -0.5
    sin_t = np.where(strict[:, None] & strict[None, :], np.sin(ang), 0.0) * n ** -0.5
    d = np.arange(FOURIER_GROUP, dtype=np.int64)
    ang = ((d[:, None] * d[None, :]) % FOURIER_GROUP) * (2.0 * np.pi / FOURIER_GROUP)
    pair = np.eye(2)
    chan_c = np.kron(pair, np.cos(ang)) * FOURIER_GROUP ** -0.5
    chan_s = np.kron(pair, np.sin(ang)) * FOURIER_GROUP ** -0.5
    xch = np.zeros((XCH, XCH_WIN))
    xch[np.arange(XCH), XCH - np.arange(XCH)] = 1.0
    xch_meta = np.zeros((N_META, N_META))
    xch_meta[np.arange(1, N_META), N_META - np.arange(1, N_META)] = 1.0
    return [t.astype(np.float32) for t in (xch, xch_meta, chan_c, chan_s, cos_t, sin_t)]


def _fourier(f, tables, seq, layer, stacked_weights):
    n_seq, n_meta = f[0].shape[0], f[1].shape[0]
    n_steps = n_seq // seq
    assert seq % (2 * XCH) == 0
    half_rows = N_META + seq // 2
    per_seq = pl.BlockSpec((seq, FOURIER_WIDTH), lambda b: (b, 0))
    per_meta = pl.BlockSpec((N_META, FOURIER_WIDTH), lambda b: (b, 0))
    w_in_specs, w_out_specs, w_out_shapes = zip(*[_cast_specs(w, layer, n_steps) for w in stacked_weights])
    w_in_specs, w_out_specs, w_out_shapes = list(w_in_specs), list(w_out_specs), list(w_out_shapes)
    outs = pl.pallas_call(
        _fourier_kernel,
        out_shape=[jax.ShapeDtypeStruct((n_seq, FOURIER_WIDTH), BF16),
                   jax.ShapeDtypeStruct((n_meta, FOURIER_WIDTH), BF16)] + w_out_shapes,
        grid=(n_steps,),
        in_specs=[per_seq, per_meta] + [_const_spec(t.shape) for t in tables] + w_in_specs,
        out_specs=[per_seq, per_meta] + w_out_specs,
        scratch_shapes=[pltpu.VMEM((half_rows, FOURIER_WIDTH), BF16)] * 5,
        compiler_params=_params(),
        name="fourier",
    )(f[0], f[1], *tables, *stacked_weights)
    return (outs[0], outs[1]), outs[2:]


def _mixout_body(in_refs, const_refs, out_refs):
    yf_ref, ya_ref, h_ref = in_refs
    gain_pre_ref, wg_ref, wf_ref, wa_ref, wo_ref, gain_ref = const_refs
    o_ref, = out_refs
    h = h_ref[...]
    u = _rms(h, gain_pre_ref[...]).astype(BF16)
    g_four = jax.nn.sigmoid(jnp.dot(u, wg_ref[:, :D_MODEL], preferred_element_type=F32))
    mixed = g_four * jnp.dot(yf_ref[...], wf_ref[...], preferred_element_type=F32)
    g_attn = jax.nn.sigmoid(jnp.dot(u, wg_ref[:, D_MODEL:], preferred_element_type=F32))
    mixed = (mixed + g_attn * jnp.dot(ya_ref[...], wa_ref[...], preferred_element_type=F32)).astype(BF16)
    out = jnp.dot(mixed, wo_ref[...], preferred_element_type=F32)
    o_ref[...] = h + _rms(out, gain_ref[...])


def _mixout(yf, ya, h, gain_pre, wg, wf, wa, wo, gain):
    seq_ins = [(yf[0], None), (ya[0], None), (h[0], None)]
    out_seq, out_meta = _token_call(_mixout_body, "mixout", seq_ins, [yf[1], ya[1], h[1]],
                                    [gain_pre, wg, wf, wa, wo, gain], (D_MODEL,), F32)
    return out_seq[0], out_meta[0]


def _ffn_body(in_refs, const_refs, out_refs):
    h_ref, = in_refs
    gain_pre_ref, wg_ref, wu_ref, wd_ref, gain_post_ref = const_refs
    o_ref, = out_refs
    h = h_ref[...]
    u = _rms(h, gain_pre_ref[...]).astype(BF16)
    ff = None
    for lo, hi in FF_CHUNKS:
        gate = jnp.dot(u, wg_ref[:, lo:hi], preferred_element_type=F32)
        up = jnp.dot(u, wu_ref[:, lo:hi], preferred_element_type=F32)
        act = (gate * jax.nn.sigmoid(gate) * up).astype(BF16)
        part = jnp.dot(act, wd_ref[lo:hi, :], preferred_element_type=F32)
        ff = part if ff is None else ff + part
    o_ref[...] = h + _rms(ff, gain_post_ref[...])


def _ffn(h, gain_pre, wg, wu, wd, gain_post):
    out_seq, out_meta = _token_call(_ffn_body, "ffn", [(h[0], None)], [h[1]],
                                    [gain_pre, wg, wu, wd, gain_post], (D_MODEL,), F32)
    return out_seq[0], out_meta[0]


def _rope_tables(positions):
    d = jnp.arange(LANES) % HEAD_DIM
    inv_freq = ROPE_THETA ** (-(2 * (d % (HEAD_DIM // 2))).astype(F32) / HEAD_DIM)
    ang = positions.astype(F32)[:, None] * inv_freq[None, :]
    sign = jnp.where(d < HEAD_DIM // 2, -1.0, 1.0).astype(F32)
    return jnp.cos(ang), jnp.sin(ang) * sign[None, :]


def _pair_heads(w, axis):
    shape = w.shape
    split = shape[:axis] + (N_KV_HEADS, Q_GROUP, HEAD_DIM) + shape[axis + 1:]
    return jnp.swapaxes(w.reshape(split), axis, axis + 1).reshape(shape)


def kernel(x, meta_tokens, w_in, w_fourier_out, w_attn_out, w_o, sink_logits, norm_mix_pre, norm_mix_post,
           norm_ffn_pre, norm_ffn_post, w_ffn_gate, w_ffn_up, w_ffn_down):
    batch, seq, _ = x.shape
    depth = w_in.shape[0]
    assert seq % ROW_TILE == 0 and seq // BLOCK >= 3

    h = (x.reshape(batch * seq, D_MODEL),
         jnp.broadcast_to(meta_tokens[None].astype(x.dtype), (batch, N_META, D_MODEL)).reshape(-1, D_MODEL))

    rope_seq = _rope_tables(N_META + jnp.arange(seq))
    rope_meta = _rope_tables(jnp.arange(batch * N_META) % N_META)
    fourier_tables = [jnp.asarray(t).astype(BF16) for t in _fourier_tables(seq)]

    gain = lambda t, l: t[l].reshape(1, D_MODEL)

    for l in range(depth):
        gate0 = IN_WIDTH - GATE_WIDTH
        w_l = jnp.concatenate([_pair_heads(w_in[l, :, :ATTN_WIDTH], 1), w_in[l, :, ATTN_WIDTH:gate0]], axis=1)
        (q, k, v, f), (qm, km, vm, fm) = _inproj(h, gain(norm_mix_pre, l), w_l.astype(BF16),
                                                 rope_seq, rope_meta, seq)
        ya, (w_gate, w_four, w_out) = _attention(sink_logits[l], (q, qm), (k, km), (v, vm), seq, l,
                                                 ((w_in, gate0), (w_fourier_out, 0), (w_o, 0)))
        yf, (wg, wu, wd) = _fourier((f, fm), fourier_tables, seq, l, (w_ffn_gate, w_ffn_up, w_ffn_down))
        h = _mixout(yf, ya, h, gain(norm_mix_pre, l), w_gate, w_four,
                    _pair_heads(w_attn_out[l], 0).astype(BF16), w_out, gain(norm_mix_post, l))
        h = _ffn(h, gain(norm_ffn_pre, l), wg, wu, wd, gain(norm_ffn_post, l))

    return h[0].reshape(batch, seq, D_MODEL)
```

```python
import functools

import jax
import jax.numpy as jnp
import numpy as np
from jax import lax
from jax.experimental import pallas as pl
from jax.experimental.pallas import tpu as pltpu

D_MODEL = 1024
N_META = 16
HEAD_DIM = 64
N_HEADS = 8
N_KV_HEADS = 2
Q_GROUP = N_HEADS // N_KV_HEADS
ATTN_WIDTH = N_HEADS * HEAD_DIM
KV_WIDTH = N_KV_HEADS * HEAD_DIM
FOURIER_WIDTH = D_MODEL - ATTN_WIDTH
FOURIER_GROUP = 128
N_FOURIER_GROUPS = FOURIER_WIDTH // FOURIER_GROUP
GATE_WIDTH = 2 * D_MODEL
IN_WIDTH = ATTN_WIDTH + 2 * KV_WIDTH + FOURIER_WIDTH + GATE_WIDTH
PROJ_WIDTHS = (ATTN_WIDTH, KV_WIDTH, KV_WIDTH, FOURIER_WIDTH)
WINDOW = 128
BLOCK = 128
ROPE_THETA = 10000.0
D_FF = 2816
EPS = 1e-6

LANES = 128
SUBLANES = 8
VMEM_LIMIT = 56 * 1024 * 1024
ROW_TILE = 1024
FF_CHUNKS = ((0, 1536), (1536, D_FF))

BF16 = jnp.bfloat16
F32 = jnp.float32


def _const_spec(shape):
    return pl.BlockSpec(shape, lambda *_: (0,) * len(shape), pipeline_mode=pl.Buffered(1))


def _params():
    return pltpu.CompilerParams(dimension_semantics=("arbitrary",), vmem_limit_bytes=VMEM_LIMIT)


def _rms(x, gain):
    return x * lax.rsqrt(jnp.mean(x * x, axis=-1, keepdims=True) + EPS) * gain


def _cast_specs(stacked, layer, n_steps, col_start=0):
    _, rows, cols = stacked.shape
    assert rows % (n_steps * 2 * SUBLANES) == 0 and col_start % LANES == 0
    block_rows = rows // n_steps
    return (pl.BlockSpec((None, block_rows, cols), lambda b: (layer, b, 0)),
            pl.BlockSpec((block_rows, cols - col_start), lambda b: (b, 0)),
            jax.ShapeDtypeStruct((rows, cols - col_start), BF16))


def _cast_blocks(src_refs, dst_refs):
    for src, dst in zip(src_refs, dst_refs):
        dst[...] = src[:, src.shape[1] - dst.shape[1]:].astype(BF16)


def _token_call(body, name, seq_ins, meta_ins, consts, out_widths, out_dtype, row_tile=ROW_TILE):
    n_seq = seq_ins[0][0].shape[0]
    n_meta = meta_ins[0].shape[0]
    n_tiles = n_seq // row_tile
    n_in, n_const, n_out = len(seq_ins), len(consts), len(out_widths)

    def kernel(*refs):
        seq_refs, refs = refs[:n_in], refs[n_in:]
        meta_refs, refs = refs[:n_in], refs[n_in:]
        const_refs, refs = refs[:n_const], refs[n_const:]
        out_seq, refs = refs[:n_out], refs[n_out:]
        out_meta = refs[:n_out]
        step = pl.program_id(0)

        @pl.when(step < n_tiles)
        def _():
            body(seq_refs, const_refs, out_seq)

        @pl.when(step == n_tiles)
        def _():
            body(meta_refs, const_refs, out_meta)

    last = n_tiles - 1
    seq_specs = []
    for arr, index_fn in seq_ins:
        index_fn = index_fn or (lambda i: i)
        seq_specs.append(pl.BlockSpec((row_tile, arr.shape[1]),
                                      functools.partial(lambda fn, i: (fn(jnp.minimum(i, last)), 0), index_fn)))
    meta_specs = [pl.BlockSpec(arr.shape, lambda i: (0, 0)) for arr in meta_ins]
    const_specs = [_const_spec(c.shape) for c in consts]
    out_shape = ([jax.ShapeDtypeStruct((n_seq, w), out_dtype) for w in out_widths]
                 + [jax.ShapeDtypeStruct((n_meta, w), out_dtype) for w in out_widths])
    out_specs = ([pl.BlockSpec((row_tile, w), lambda i: (jnp.minimum(i, last), 0)) for w in out_widths]
                 + [pl.BlockSpec((n_meta, w), lambda i: (0, 0)) for w in out_widths])
    outs = pl.pallas_call(
        kernel,
        out_shape=out_shape,
        grid=(n_tiles + 1,),
        in_specs=seq_specs + meta_specs + const_specs,
        out_specs=out_specs,
        compiler_params=_params(),
        name=name,
    )(*[a for a, _ in seq_ins], *meta_ins, *consts)
    return outs[:n_out], outs[n_out:]


def _inproj_body(in_refs, const_refs, out_refs):
    h_ref, cos_ref, sin_ref = in_refs
    gain_ref, w_ref = const_refs
    q_ref, k_ref, v_ref, f_ref = out_refs
    u = _rms(h_ref[...], gain_ref[...]).astype(BF16)
    cos = cos_ref[...]
    sin = sin_ref[...]
    first_half = (lax.broadcasted_iota(jnp.int32, (1, LANES), 1) % HEAD_DIM) < HEAD_DIM // 2

    def rope(y):
        rot = jnp.where(first_half, pltpu.roll(y, LANES - HEAD_DIM // 2, 1),
                        pltpu.roll(y, HEAD_DIM // 2, 1))
        return y * cos + rot * sin

    def proj(lo, hi):
        return jnp.dot(u, w_ref[:, lo:hi], preferred_element_type=F32)

    q = proj(0, ATTN_WIDTH)
    scale = HEAD_DIM ** -0.5
    for c in range(ATTN_WIDTH // LANES):
        q_ref[:, c * LANES:(c + 1) * LANES] = (rope(q[:, c * LANES:(c + 1) * LANES]) * scale).astype(BF16)
    kv = proj(ATTN_WIDTH, ATTN_WIDTH + 2 * KV_WIDTH)
    k_ref[...] = rope(kv[:, :KV_WIDTH]).astype(BF16)
    v_ref[...] = kv[:, KV_WIDTH:].astype(BF16)
    f0 = ATTN_WIDTH + 2 * KV_WIDTH
    f_ref[...] = proj(f0, f0 + FOURIER_WIDTH).astype(BF16)


def _inproj(h, gain, w, rope_seq, rope_meta, seq):
    table_index = lambda i: 0
    seq_ins = [(h[0], None), (rope_seq[0], table_index), (rope_seq[1], table_index)]
    return _token_call(_inproj_body, "inproj", seq_ins, [h[1], rope_meta[0], rope_meta[1]], [gain, w],
                       PROJ_WIDTHS, BF16, row_tile=seq)


MASKED = -1e30


def _attend(q_rows, k_all, v_ones, bias, sink_ref):
    r = q_rows.shape[0]
    low_lanes = lax.broadcasted_iota(jnp.int32, (1, LANES), 1) < HEAD_DIM
    top_rows = lax.broadcasted_iota(jnp.int32, (2 * r, 1), 0) < r
    bias2 = jnp.concatenate([bias, bias], axis=0)
    zero = jnp.zeros((), BF16)
    outs = []
    for j in range(Q_GROUP):
        qj = q_rows[:, j * LANES:(j + 1) * LANES]
        qm = jnp.concatenate([jnp.where(low_lanes, qj, zero), jnp.where(low_lanes, zero, qj)], axis=0)
        s = lax.dot_general(qm, k_all, (((1,), (1,)), ((), ())), preferred_element_type=F32) + bias2
        sink = jnp.where(top_rows, sink_ref[j], sink_ref[Q_GROUP + j])
        m = jnp.maximum(jnp.max(s, axis=-1, keepdims=True), sink)
        p = jnp.exp((s - m).astype(BF16))
        o = jnp.dot(p, v_ones, preferred_element_type=F32)
        o = o[:, :LANES] / (o[:, LANES:] + jnp.exp(sink - m))
        outs.append(jnp.where(low_lanes, o[:r], o[r:]))
    return jnp.concatenate(outs, axis=1)


def _attn_kernel(sink_ref, q_ref, k_ref, v_ref, qm_ref, km_ref, vm_ref, bias_ref, bias_meta_ref,
                 wa_ref, wb_ref, wc_ref, o_ref, om_ref, wa_out, wb_out, wc_out):
    _cast_blocks((wa_ref, wb_ref, wc_ref), (wa_out, wb_out, wc_out))
    seq = q_ref.shape[0]
    n_blocks = seq // BLOCK
    band = 3 * BLOCK

    def with_ones(v):
        return jnp.concatenate([v, jnp.ones_like(v)], axis=1)

    n_first = bias_meta_ref.shape[1] - N_META
    k_first = jnp.concatenate([km_ref[...], k_ref[0:n_first, :]], axis=0)
    v_first = jnp.concatenate([vm_ref[...], v_ref[0:n_first, :]], axis=0)
    om_ref[...] = _attend(qm_ref[...], k_first, with_ones(v_first), bias_meta_ref[...], sink_ref).astype(BF16)

    k_head = k_first[0:BLOCK]
    v_head = v_first[0:BLOCK]

    def body(i, carry):
        q_start = pl.multiple_of(BLOCK * i, BLOCK)
        kb = jnp.clip(i - 1, 0, n_blocks - 3)
        k_start = pl.multiple_of(BLOCK * kb, BLOCK)
        kind = jnp.where(i == 0, 0, jnp.where(i == n_blocks - 1, 2, 1))
        k_all = jnp.concatenate([k_head, k_ref[pl.ds(k_start, band), :]], axis=0)
        v_all = jnp.concatenate([v_head, v_ref[pl.ds(k_start, band), :]], axis=0)
        o = _attend(q_ref[pl.ds(q_start, BLOCK), :], k_all, with_ones(v_all), bias_ref[kind], sink_ref)
        o_ref[pl.ds(q_start, BLOCK), :] = o.astype(BF16)
        return carry

    lax.fori_loop(0, n_blocks, body, 0, unroll=16)


def _attention_bias():
    row = np.arange(BLOCK)[:, None]
    col = np.arange(BLOCK + 3 * BLOCK)[None, :]
    kinds = []
    for first_band_block in (0, -1, -2):
        rel = (col - BLOCK) + BLOCK * first_band_block - row
        kinds.append((col < N_META) | ((col >= BLOCK) & (np.abs(rel) <= WINDOW)))
    p_idx = np.arange(N_META)[:, None]
    col = np.arange(2 * BLOCK)[None, :]
    meta = (col < N_META) | (col - p_idx <= WINDOW)
    to_bias = lambda visible: np.where(visible, 0.0, MASKED).astype(np.float32)
    return to_bias(np.stack(kinds)), to_bias(meta)


def _attention(sink, q, k, v, seq, layer, cast_jobs):
    n_seq, n_meta = q[0].shape[0], q[1].shape[0]
    n_steps = n_seq // seq
    bias, bias_meta = _attention_bias()
    per_seq = lambda width: pl.BlockSpec((seq, width), lambda b: (b, 0))
    per_meta = lambda width: pl.BlockSpec((N_META, width), lambda b: (b, 0))
    widths = (ATTN_WIDTH, KV_WIDTH, KV_WIDTH)
    w_in_specs, w_out_specs, w_out_shapes = zip(*[_cast_specs(w, layer, n_steps, c0) for w, c0 in cast_jobs])
    outs = pl.pallas_call(
        _attn_kernel,
        out_shape=[jax.ShapeDtypeStruct((n_seq, ATTN_WIDTH), BF16),
                   jax.ShapeDtypeStruct((n_meta, ATTN_WIDTH), BF16)] + list(w_out_shapes),
        grid=(n_steps,),
        in_specs=([pl.BlockSpec(memory_space=pltpu.SMEM)] + [per_seq(w) for w in widths]
                  + [per_meta(w) for w in widths] + [_const_spec(bias.shape), _const_spec(bias_meta.shape)]
                  + list(w_in_specs)),
        out_specs=[per_seq(ATTN_WIDTH), per_meta(ATTN_WIDTH)] + list(w_out_specs),
        compiler_params=_params(),
        name="attn",
    )(sink, q[0], k[0], v[0], q[1], k[1], v[1], bias, bias_meta, *[w for w, _ in cast_jobs])
    return (outs[0], outs[1]), outs[2:]


XCH = 128
XCH_WIN = XCH + N_META


def _fourier_kernel(f_ref, fm_ref, xch_ref, xch_meta_ref, chan_c_ref, chan_s_ref, cos_ref, sin_ref,
                    wa_ref, wb_ref, wc_ref, y_ref, ym_ref, wa_out, wb_out, wc_out,
                    ex_ref, ox_ref, ec_ref, os_ref, t_ref):
    _cast_blocks((wa_ref, wb_ref, wc_ref), (wa_out, wb_out, wc_out))

    seq = f_ref.shape[0]
    half = seq // 2
    centre = half - N_META // 2
    f32_dot = functools.partial(jnp.dot, preferred_element_type=F32)

    fm = fm_ref[...].astype(F32)
    gm = f32_dot(xch_meta_ref[...], f_ref[seq - N_META:seq, :])
    ex_ref[0:N_META, :] = (fm + gm).astype(BF16)
    ox_ref[0:N_META, :] = (fm - gm).astype(BF16)
    for c in range(half // XCH):
        w = seq - XCH_WIN - XCH * c
        g = f32_dot(xch_ref[...], f_ref[w:w + XCH_WIN, :])
        fs = f_ref[XCH * c:XCH * (c + 1), :].astype(F32)
        if XCH * (c + 1) > centre:
            paired = XCH * c + lax.broadcasted_iota(jnp.int32, (XCH, 1), 0) < centre
            e = fs + jnp.where(paired, g, 0.0)
            o = jnp.where(paired, fs - g, 0.0)
        else:
            e, o = fs + g, fs - g
        rows = slice(N_META + XCH * c, N_META + XCH * (c + 1))
        ex_ref[rows, :] = e.astype(BF16)
        ox_ref[rows, :] = o.astype(BF16)

    for p in range(FOURIER_WIDTH // (2 * FOURIER_GROUP)):
        lanes = slice(2 * FOURIER_GROUP * p, 2 * FOURIER_GROUP * (p + 1))
        ec_ref[:, lanes] = f32_dot(ex_ref[:, lanes], chan_c_ref[...]).astype(BF16)
        os_ref[:, lanes] = f32_dot(ox_ref[:, lanes], chan_s_ref[...]).astype(BF16)

    a = f32_dot(cos_ref[...], ec_ref[...])
    b = f32_dot(sin_ref[...], os_ref[...])
    u = a - b
    t_ref[...] = (a + b).astype(BF16)
    ym_ref[...] = u[0:N_META].astype(BF16)
    y_ref[0:half - N_META, :] = u[N_META:half].astype(BF16)
    mirrored = f32_dot(xch_meta_ref[...], t_ref[half:half + N_META, :])
    low = lax.broadcasted_iota(jnp.int32, (N_META, 1), 0) < N_META // 2
    y_ref[half - N_META:half, :] = jnp.where(low, u[half:half + N_META], mirrored).astype(BF16)
    for c in range(half // XCH):
        start = half + XCH * c
        w = seq - start - XCH
        y_ref[start:start + XCH, :] = f32_dot(xch_ref[...], t_ref[w:w + XCH_WIN, :]).astype(BF16)


def _fourier_tables(seq):
    n = N_META + seq
    h = n // 2
    rows = N_META + seq // 2
    idx = np.arange(rows, dtype=np.int64)
    ang = ((idx[:, None] * idx[None, :]) % n) * (2.0 * np.pi / n)
    inside = idx <= h
    strict = (idx >= 1) & (idx < h)
    cos_t = np.where(inside[:, None] & inside[None, :], np.cos(ang), 0.0) * n ** -0.5
    sin_t = np.where(strict[:, None] & strict[None, :], np.sin(ang), 0.0) * n ** -0.5
    d = np.arange(FOURIER_GROUP, dtype=np.int64)
    ang = ((d[:, None] * d[None, :]) % FOURIER_GROUP) * (2.0 * np.pi / FOURIER_GROUP)
    pair = np.eye(2)
    chan_c = np.kron(pair, np.cos(ang)) * FOURIER_GROUP ** -0.5
    chan_s = np.kron(pair, np.sin(ang)) * FOURIER_GROUP ** -0.5
    xch = np.zeros((XCH, XCH_WIN))
    xch[np.arange(XCH), XCH - np.arange(XCH)] = 1.0
    xch_meta = np.zeros((N_META, N_META))
    xch_meta[np.arange(1, N_META), N_META - np.arange(1, N_META)] = 1.0
    return [t.astype(np.float32) for t in (xch, xch_meta, chan_c, chan_s, cos_t, sin_t)]


def _fourier(f, tables, seq, layer, stacked_weights):
    n_seq, n_meta = f[0].shape[0], f[1].shape[0]
    n_steps = n_seq // seq
    assert seq % (2 * XCH) == 0
    half_rows = N_META + seq // 2
    per_seq = pl.BlockSpec((seq, FOURIER_WIDTH), lambda b: (b, 0))
    per_meta = pl.BlockSpec((N_META, FOURIER_WIDTH), lambda b: (b, 0))
    w_in_specs, w_out_specs, w_out_shapes = zip(*[_cast_specs(w, layer, n_steps) for w in stacked_weights])
    w_in_specs, w_out_specs, w_out_shapes = list(w_in_specs), list(w_out_specs), list(w_out_shapes)
    outs = pl.pallas_call(
        _fourier_kernel,
        out_shape=[jax.ShapeDtypeStruct((n_seq, FOURIER_WIDTH), BF16),
                   jax.ShapeDtypeStruct((n_meta, FOURIER_WIDTH), BF16)] + w_out_shapes,
        grid=(n_steps,),
        in_specs=[per_seq, per_meta] + [_const_spec(t.shape) for t in tables] + w_in_specs,
        out_specs=[per_seq, per_meta] + w_out_specs,
        scratch_shapes=[pltpu.VMEM((half_rows, FOURIER_WIDTH), BF16)] * 5,
        compiler_params=_params(),
        name="fourier",
    )(f[0], f[1], *tables, *stacked_weights)
    return (outs[0], outs[1]), outs[2:]


def _mixout_body(in_refs, const_refs, out_refs):
    yf_ref, ya_ref, h_ref = in_refs
    gain_pre_ref, wg_ref, wf_ref, wa_ref, wo_ref, gain_ref = const_refs
    o_ref, = out_refs
    h = h_ref[...]
    u = _rms(h, gain_pre_ref[...]).astype(BF16)
    g_four = jax.nn.sigmoid(jnp.dot(u, wg_ref[:, :D_MODEL], preferred_element_type=F32))
    mixed = g_four * jnp.dot(yf_ref[...], wf_ref[...], preferred_element_type=F32)
    g_attn = jax.nn.sigmoid(jnp.dot(u, wg_ref[:, D_MODEL:], preferred_element_type=F32))
    mixed = (mixed + g_attn * jnp.dot(ya_ref[...], wa_ref[...], preferred_element_type=F32)).astype(BF16)
    out = jnp.dot(mixed, wo_ref[...], preferred_element_type=F32)
    o_ref[...] = h + _rms(out, gain_ref[...])


def _mixout(yf, ya, h, gain_pre, wg, wf, wa, wo, gain):
    seq_ins = [(yf[0], None), (ya[0], None), (h[0], None)]
    out_seq, out_meta = _token_call(_mixout_body, "mixout", seq_ins, [yf[1], ya[1], h[1]],
                                    [gain_pre, wg, wf, wa, wo, gain], (D_MODEL,), F32)
    return out_seq[0], out_meta[0]


def _ffn_body(in_refs, const_refs, out_refs):
    h_ref, = in_refs
    gain_pre_ref, wg_ref, wu_ref, wd_ref, gain_post_ref = const_refs
    o_ref, = out_refs
    h = h_ref[...]
    u = _rms(h, gain_pre_ref[...]).astype(BF16)
    ff = None
    for lo, hi in FF_CHUNKS:
        gate = jnp.dot(u, wg_ref[:, lo:hi], preferred_element_type=F32)
        up = jnp.dot(u, wu_ref[:, lo:hi], preferred_element_type=F32)
        act = (gate * jax.nn.sigmoid(gate) * up).astype(BF16)
        part = jnp.dot(act, wd_ref[lo:hi, :], preferred_element_type=F32)
        ff = part if ff is None else ff + part
    o_ref[...] = h + _rms(ff, gain_post_ref[...])


def _ffn(h, gain_pre, wg, wu, wd, gain_post):
    out_seq, out_meta = _token_call(_ffn_body, "ffn", [(h[0], None)], [h[1]],
                                    [gain_pre, wg, wu, wd, gain_post], (D_MODEL,), F32)
    return out_seq[0], out_meta[0]


def _rope_tables(positions):
    d = jnp.arange(LANES) % HEAD_DIM
    inv_freq = ROPE_THETA ** (-(2 * (d % (HEAD_DIM // 2))).astype(F32) / HEAD_DIM)
    ang = positions.astype(F32)[:, None] * inv_freq[None, :]
    sign = jnp.where(d < HEAD_DIM // 2, -1.0, 1.0).astype(F32)
    return jnp.cos(ang), jnp.sin(ang) * sign[None, :]


def _pair_heads(w, axis):
    shape = w.shape
    split = shape[:axis] + (N_KV_HEADS, Q_GROUP, HEAD_DIM) + shape[axis + 1:]
    return jnp.swapaxes(w.reshape(split), axis, axis + 1).reshape(shape)


def kernel(x, meta_tokens, w_in, w_fourier_out, w_attn_out, w_o, sink_logits, norm_mix_pre, norm_mix_post,
           norm_ffn_pre, norm_ffn_post, w_ffn_gate, w_ffn_up, w_ffn_down):
    batch, seq, _ = x.shape
    depth = w_in.shape[0]
    assert seq % ROW_TILE == 0 and seq // BLOCK >= 3

    h = (x.reshape(batch * seq, D_MODEL),
         jnp.broadcast_to(meta_tokens[None].astype(x.dtype), (batch, N_META, D_MODEL)).reshape(-1, D_MODEL))

    rope_seq = _rope_tables(N_META + jnp.arange(seq))
    rope_meta = _rope_tables(jnp.arange(batch * N_META) % N_META)
    fourier_tables = [jnp.asarray(t).astype(BF16) for t in _fourier_tables(seq)]

    gain = lambda t, l: t[l].reshape(1, D_MODEL)

    for l in range(depth):
        gate0 = IN_WIDTH - GATE_WIDTH
        w_l = jnp.concatenate([_pair_heads(w_in[l, :, :ATTN_WIDTH], 1), w_in[l, :, ATTN_WIDTH:gate0]], axis=1)
        (q, k, v, f), (qm, km, vm, fm) = _inproj(h, gain(norm_mix_pre, l), w_l.astype(BF16),
                                                 rope_seq, rope_meta, seq)
        ya, (w_gate, w_four, w_out) = _attention(sink_logits[l], (q, qm), (k, km), (v, vm), seq, l,
                                                 ((w_in, gate0), (w_fourier_out, 0), (w_o, 0)))
        yf, (wg, wu, wd) = _fourier((f, fm), fourier_tables, seq, l, (w_ffn_gate, w_ffn_up, w_ffn_down))
        h = _mixout(yf, ya, h, gain(norm_mix_pre, l), w_gate, w_four,
                    _pair_heads(w_attn_out[l], 0).astype(BF16), w_out, gain(norm_mix_post, l))
        h = _ffn(h, gain(norm_ffn_pre, l), wg, wu, wd, gain(norm_ffn_post, l))

    return h[0].reshape(batch, seq, D_MODEL)
```

```python
import functools

import jax
import jax.numpy as jnp
import numpy as np
from jax import lax
from jax.experimental import pallas as pl
from jax.experimental.pallas import tpu as pltpu

D_MODEL = 1024
N_META = 16
HEAD_DIM = 64
N_HEADS = 8
N_KV_HEADS = 2
Q_GROUP = N_HEADS // N_KV_HEADS
ATTN_WIDTH = N_HEADS * HEAD_DIM
KV_WIDTH = N_KV_HEADS * HEAD_DIM
FOURIER_WIDTH = D_MODEL - ATTN_WIDTH
FOURIER_GROUP = 128
N_FOURIER_GROUPS = FOURIER_WIDTH // FOURIER_GROUP
GATE_WIDTH = 2 * D_MODEL
IN_WIDTH = ATTN_WIDTH + 2 * KV_WIDTH + FOURIER_WIDTH + GATE_WIDTH
PROJ_WIDTHS = (ATTN_WIDTH, KV_WIDTH, KV_WIDTH, FOURIER_WIDTH)
WINDOW = 128
BLOCK = 128
ROPE_THETA = 10000.0
D_FF = 2816
EPS = 1e-6

LANES = 128
SUBLANES = 8
VMEM_LIMIT = 56 * 1024 * 1024
ROW_TILE = 1024
BATCH_PER_STEP = 2
FF_CHUNKS = ((0, 1536), (1536, D_FF))

BF16 = jnp.bfloat16
F32 = jnp.float32


def _const_spec(shape):
    return pl.BlockSpec(shape, lambda *_: (0,) * len(shape), pipeline_mode=pl.Buffered(1))


def _params():
    return pltpu.CompilerParams(dimension_semantics=("arbitrary",), vmem_limit_bytes=VMEM_LIMIT)


def _rms(x, gain):
    return x * lax.rsqrt(jnp.mean(x * x, axis=-1, keepdims=True) + EPS) * gain


def _cast_specs(stacked, layer, n_steps, col_start=0):
    _, rows, cols = stacked.shape
    assert rows % (n_steps * 2 * SUBLANES) == 0 and col_start % LANES == 0
    block_rows = rows // n_steps
    return (pl.BlockSpec((None, block_rows, cols), lambda b: (layer, b, 0)),
            pl.BlockSpec((block_rows, cols - col_start), lambda b: (b, 0)),
            jax.ShapeDtypeStruct((rows, cols - col_start), BF16))


def _cast_blocks(src_refs, dst_refs):
    for src, dst in zip(src_refs, dst_refs):
        dst[...] = src[:, src.shape[1] - dst.shape[1]:].astype(BF16)


def _token_call(body, name, seq_ins, meta_ins, consts, out_widths, out_dtype, row_tile=ROW_TILE):
    n_seq = seq_ins[0][0].shape[0]
    n_meta = meta_ins[0].shape[0]
    n_tiles = n_seq // row_tile
    n_in, n_const, n_out = len(seq_ins), len(consts), len(out_widths)

    def kernel(*refs):
        seq_refs, refs = refs[:n_in], refs[n_in:]
        meta_refs, refs = refs[:n_in], refs[n_in:]
        const_refs, refs = refs[:n_const], refs[n_const:]
        out_seq, refs = refs[:n_out], refs[n_out:]
        out_meta = refs[:n_out]
        step = pl.program_id(0)

        @pl.when(step < n_tiles)
        def _():
            body(seq_refs, const_refs, out_seq)

        @pl.when(step == n_tiles)
        def _():
            body(meta_refs, const_refs, out_meta)

    last = n_tiles - 1
    seq_specs = []
    for arr, index_fn in seq_ins:
        index_fn = index_fn or (lambda i: i)
        seq_specs.append(pl.BlockSpec((row_tile, arr.shape[1]),
                                      functools.partial(lambda fn, i: (fn(jnp.minimum(i, last)), 0), index_fn)))
    meta_specs = [pl.BlockSpec(arr.shape, lambda i: (0, 0)) for arr in meta_ins]
    const_specs = [_const_spec(c.shape) for c in consts]
    out_shape = ([jax.ShapeDtypeStruct((n_seq, w), out_dtype) for w in out_widths]
                 + [jax.ShapeDtypeStruct((n_meta, w), out_dtype) for w in out_widths])
    out_specs = ([pl.BlockSpec((row_tile, w), lambda i: (jnp.minimum(i, last), 0)) for w in out_widths]
                 + [pl.BlockSpec((n_meta, w), lambda i: (0, 0)) for w in out_widths])
    outs = pl.pallas_call(
        kernel,
        out_shape=out_shape,
        grid=(n_tiles + 1,),
        in_specs=seq_specs + meta_specs + const_specs,
        out_specs=out_specs,
        compiler_params=_params(),
        name=name,
    )(*[a for a, _ in seq_ins], *meta_ins, *consts)
    return outs[:n_out], outs[n_out:]


def _inproj_body(in_refs, const_refs, out_refs):
    h_ref, cos_ref, sin_ref = in_refs
    gain_ref, w_ref = const_refs
    q_ref, k_ref, v_ref, f_ref = out_refs
    u = _rms(h_ref[...], gain_ref[...]).astype(BF16)
    cos = cos_ref[...]
    sin = sin_ref[...]
    first_half = (lax.broadcasted_iota(jnp.int32, (1, LANES), 1) % HEAD_DIM) < HEAD_DIM // 2

    def rope(y):
        rot = jnp.where(first_half, pltpu.roll(y, LANES - HEAD_DIM // 2, 1),
                        pltpu.roll(y, HEAD_DIM // 2, 1))
        return y * cos + rot * sin

    def proj(lo, hi):
        return jnp.dot(u, w_ref[:, lo:hi], preferred_element_type=F32)

    q = proj(0, ATTN_WIDTH)
    scale = HEAD_DIM ** -0.5
    for c in range(ATTN_WIDTH // LANES):
        q_ref[:, c * LANES:(c + 1) * LANES] = (rope(q[:, c * LANES:(c + 1) * LANES]) * scale).astype(BF16)
    kv = proj(ATTN_WIDTH, ATTN_WIDTH + 2 * KV_WIDTH)
    k_ref[...] = rope(kv[:, :KV_WIDTH]).astype(BF16)
    v_ref[...] = kv[:, KV_WIDTH:].astype(BF16)
    f0 = ATTN_WIDTH + 2 * KV_WIDTH
    f_ref[...] = proj(f0, f0 + FOURIER_WIDTH).astype(BF16)


def _inproj(h, gain, w, rope_seq, rope_meta, seq):
    table_index = lambda i: 0
    seq_ins = [(h[0], None), (rope_seq[0], table_index), (rope_seq[1], table_index)]
    return _token_call(_inproj_body, "inproj", seq_ins, [h[1], rope_meta[0], rope_meta[1]], [gain, w],
                       PROJ_WIDTHS, BF16, row_tile=seq)


MASKED = -1e30


def _attend(q_rows, k_all, v_ones, bias, sink_ref):
    r = q_rows.shape[0]
    low_lanes = lax.broadcasted_iota(jnp.int32, (1, LANES), 1) < HEAD_DIM
    top_rows = lax.broadcasted_iota(jnp.int32, (2 * r, 1), 0) < r
    bias2 = jnp.concatenate([bias, bias], axis=0)
    zero = jnp.zeros((), BF16)
    outs = []
    for j in range(Q_GROUP):
        qj = q_rows[:, j * LANES:(j + 1) * LANES]
        qm = jnp.concatenate([jnp.where(low_lanes, qj, zero), jnp.where(low_lanes, zero, qj)], axis=0)
        s = lax.dot_general(qm, k_all, (((1,), (1,)), ((), ())), preferred_element_type=F32) + bias2
        sink = jnp.where(top_rows, sink_ref[j], sink_ref[Q_GROUP + j])
        m = jnp.maximum(jnp.max(s, axis=-1, keepdims=True), sink)
        p = jnp.exp((s - m).astype(BF16))
        o = jnp.dot(p, v_ones, preferred_element_type=F32)
        o = o[:, :LANES] / (o[:, LANES:] + jnp.exp(sink - m))
        outs.append(jnp.where(low_lanes, o[:r], o[r:]))
    return jnp.concatenate(outs, axis=1)


def _per_batch(fn, seq_refs, meta_refs):
    for sub in range(BATCH_PER_STEP):
        seq = seq_refs[0].shape[0] // BATCH_PER_STEP
        fn([r.at[pl.ds(sub * seq, seq)] for r in seq_refs],
           [r.at[pl.ds(sub * N_META, N_META)] for r in meta_refs])


def _attn_kernel(sink_ref, q_ref, k_ref, v_ref, qm_ref, km_ref, vm_ref, bias_ref, bias_meta_ref,
                 wa_ref, wb_ref, wc_ref, o_ref, om_ref, wa_out, wb_out, wc_out):
    _cast_blocks((wa_ref, wb_ref, wc_ref), (wa_out, wb_out, wc_out))
    _per_batch(lambda s, m: _attn_one(sink_ref, s[0], s[1], s[2], m[0], m[1], m[2], bias_ref, bias_meta_ref,
                                      s[3], m[3]),
               (q_ref, k_ref, v_ref, o_ref), (qm_ref, km_ref, vm_ref, om_ref))


def _attn_one(sink_ref, q_ref, k_ref, v_ref, qm_ref, km_ref, vm_ref, bias_ref, bias_meta_ref, o_ref, om_ref):
    seq = q_ref.shape[0]
    n_blocks = seq // BLOCK
    band = 3 * BLOCK

    def with_ones(v):
        return jnp.concatenate([v, jnp.ones_like(v)], axis=1)

    n_first = bias_meta_ref.shape[1] - N_META
    k_first = jnp.concatenate([km_ref[...], k_ref[0:n_first, :]], axis=0)
    v_first = jnp.concatenate([vm_ref[...], v_ref[0:n_first, :]], axis=0)
    om_ref[...] = _attend(qm_ref[...], k_first, with_ones(v_first), bias_meta_ref[...], sink_ref).astype(BF16)

    k_head = k_first[0:BLOCK]
    v_head = v_first[0:BLOCK]

    def body(i, carry):
        q_start = pl.multiple_of(BLOCK * i, BLOCK)
        kb = jnp.clip(i - 1, 0, n_blocks - 3)
        k_start = pl.multiple_of(BLOCK * kb, BLOCK)
        kind = jnp.where(i == 0, 0, jnp.where(i == n_blocks - 1, 2, 1))
        k_all = jnp.concatenate([k_head, k_ref[pl.ds(k_start, band), :]], axis=0)
        v_all = jnp.concatenate([v_head, v_ref[pl.ds(k_start, band), :]], axis=0)
        o = _attend(q_ref[pl.ds(q_start, BLOCK), :], k_all, with_ones(v_all), bias_ref[kind], sink_ref)
        o_ref[pl.ds(q_start, BLOCK), :] = o.astype(BF16)
        return carry

    lax.fori_loop(0, n_blocks, body, 0, unroll=16)


def _attention_bias():
    row = np.arange(BLOCK)[:, None]
    col = np.arange(BLOCK + 3 * BLOCK)[None, :]
    kinds = []
    for first_band_block in (0, -1, -2):
        rel = (col - BLOCK) + BLOCK * first_band_block - row
        kinds.append((col < N_META) | ((col >= BLOCK) & (np.abs(rel) <= WINDOW)))
    p_idx = np.arange(N_META)[:, None]
    col = np.arange(2 * BLOCK)[None, :]
    meta = (col < N_META) | (col - p_idx <= WINDOW)
    to_bias = lambda visible: np.where(visible, 0.0, MASKED).astype(np.float32)
    return to_bias(np.stack(kinds)), to_bias(meta)


def _attention(sink, q, k, v, seq, layer, cast_jobs):
    n_seq, n_meta = q[0].shape[0], q[1].shape[0]
    n_steps = n_seq // (BATCH_PER_STEP * seq)
    bias, bias_meta = _attention_bias()
    per_seq = lambda width: pl.BlockSpec((BATCH_PER_STEP * seq, width), lambda b: (b, 0))
    per_meta = lambda width: pl.BlockSpec((BATCH_PER_STEP * N_META, width), lambda b: (b, 0))
    widths = (ATTN_WIDTH, KV_WIDTH, KV_WIDTH)
    w_in_specs, w_out_specs, w_out_shapes = zip(*[_cast_specs(w, layer, n_steps, c0) for w, c0 in cast_jobs])
    outs = pl.pallas_call(
        _attn_kernel,
        out_shape=[jax.ShapeDtypeStruct((n_seq, ATTN_WIDTH), BF16),
                   jax.ShapeDtypeStruct((n_meta, ATTN_WIDTH), BF16)] + list(w_out_shapes),
        grid=(n_steps,),
        in_specs=([pl.BlockSpec(memory_space=pltpu.SMEM)] + [per_seq(w) for w in widths]
                  + [per_meta(w) for w in widths] + [_const_spec(bias.shape), _const_spec(bias_meta.shape)]
                  + list(w_in_specs)),
        out_specs=[per_seq(ATTN_WIDTH), per_meta(ATTN_WIDTH)] + list(w_out_specs),
        compiler_params=_params(),
        name="attn",
    )(sink, q[0], k[0], v[0], q[1], k[1], v[1], bias, bias_meta, *[w for w, _ in cast_jobs])
    return (outs[0], outs[1]), outs[2:]


XCH = 128
XCH_WIN = XCH + N_META


def _fourier_kernel(f_ref, fm_ref, xch_ref, xch_meta_ref, chan_c_ref, chan_s_ref, cos_ref, sin_ref,
                    wa_ref, wb_ref, wc_ref, y_ref, ym_ref, wa_out, wb_out, wc_out,
                    ex_ref, ox_ref, ec_ref, os_ref, t_ref):
    _cast_blocks((wa_ref, wb_ref, wc_ref), (wa_out, wb_out, wc_out))
    tables = (xch_ref, xch_meta_ref, chan_c_ref, chan_s_ref, cos_ref, sin_ref)
    scratch = (ex_ref, ox_ref, ec_ref, os_ref, t_ref)
    _per_batch(lambda s, m: _fourier_one(s[0], m[0], *tables, s[1], m[1], *scratch),
               (f_ref, y_ref), (fm_ref, ym_ref))


def _fourier_one(f_ref, fm_ref, xch_ref, xch_meta_ref, chan_c_ref, chan_s_ref, cos_ref, sin_ref,
                 y_ref, ym_ref, ex_ref, ox_ref, ec_ref, os_ref, t_ref):
    seq = f_ref.shape[0]
    half = seq // 2
    centre = half - N_META // 2
    f32_dot = functools.partial(jnp.dot, preferred_element_type=F32)

    fm = fm_ref[...].astype(F32)
    gm = f32_dot(xch_meta_ref[...], f_ref[seq - N_META:seq, :])
    ex_ref[0:N_META, :] = (fm + gm).astype(BF16)
    ox_ref[0:N_META, :] = (fm - gm).astype(BF16)
    for c in range(half // XCH):
        w = seq - XCH_WIN - XCH * c
        g = f32_dot(xch_ref[...], f_ref[w:w + XCH_WIN, :])
        fs = f_ref[XCH * c:XCH * (c + 1), :].astype(F32)
        if XCH * (c + 1) > centre:
            paired = XCH * c + lax.broadcasted_iota(jnp.int32, (XCH, 1), 0) < centre
            e = fs + jnp.where(paired, g, 0.0)
            o = jnp.where(paired, fs - g, 0.0)
        else:
            e, o = fs + g, fs - g
        rows = slice(N_META + XCH * c, N_META + XCH * (c + 1))
        ex_ref[rows, :] = e.astype(BF16)
        ox_ref[rows, :] = o.astype(BF16)

    for p in range(FOURIER_WIDTH // (2 * FOURIER_GROUP)):
        lanes = slice(2 * FOURIER_GROUP * p, 2 * FOURIER_GROUP * (p + 1))
        ec_ref[:, lanes] = f32_dot(ex_ref[:, lanes], chan_c_ref[...]).astype(BF16)
        os_ref[:, lanes] = f32_dot(ox_ref[:, lanes], chan_s_ref[...]).astype(BF16)

    a = f32_dot(cos_ref[...], ec_ref[...])
    b = f32_dot(sin_ref[...], os_ref[...])
    u = a - b
    t_ref[...] = (a + b).astype(BF16)
    ym_ref[...] = u[0:N_META].astype(BF16)
    y_ref[0:half - N_META, :] = u[N_META:half].astype(BF16)
    mirrored = f32_dot(xch_meta_ref[...], t_ref[half:half + N_META, :])
    low = lax.broadcasted_iota(jnp.int32, (N_META, 1), 0) < N_META // 2
    y_ref[half - N_META:half, :] = jnp.where(low, u[half:half + N_META], mirrored).astype(BF16)
    for c in range(half // XCH):
        start = half + XCH * c
        w = seq - start - XCH
        y_ref[start:start + XCH, :] = f32_dot(xch_ref[...], t_ref[w:w + XCH_WIN, :]).astype(BF16)


def _fourier_tables(seq):
    n = N_META + seq
    h = n // 2
    rows = N_META + seq // 2
    idx = np.arange(rows, dtype=np.int64)
    ang = ((idx[:, None] * idx[None, :]) % n) * (2.0 * np.pi / n)
    inside = idx <= h
    strict = (idx >= 1) & (idx < h)
    cos_t = np.where(inside[:, None] & inside[None, :], np.cos(ang), 0.0) * n ** -0.5
    sin_t = np.where(strict[:, None] & strict[None, :], np.sin(ang), 0.0) * n ** -0.5
    d = np.arange(FOURIER_GROUP, dtype=np.int64)
    ang = ((d[:, None] * d[None, :]) % FOURIER_GROUP) * (2.0 * np.pi / FOURIER_GROUP)
    pair = np.eye(2)
    chan_c = np.kron(pair, np.cos(ang)) * FOURIER_GROUP ** -0.5
    chan_s = np.kron(pair, np.sin(ang)) * FOURIER_GROUP ** -0.5
    xch = np.zeros((XCH, XCH_WIN))
    xch[np.arange(XCH), XCH - np.arange(XCH)] = 1.0
    xch_meta = np.zeros((N_META, N_META))
    xch_meta[np.arange(1, N_META), N_META - np.arange(1, N_META)] = 1.0
    return [t.astype(np.float32) for t in (xch, xch_meta, chan_c, chan_s, cos_t, sin_t)]


def _fourier(f, tables, seq, layer, stacked_weights):
    n_seq, n_meta = f[0].shape[0], f[1].shape[0]
    n_steps = n_seq // (BATCH_PER_STEP * seq)
    assert seq % (2 * XCH) == 0
    half_rows = N_META + seq // 2
    per_seq = pl.BlockSpec((BATCH_PER_STEP * seq, FOURIER_WIDTH), lambda b: (b, 0))
    per_meta = pl.BlockSpec((BATCH_PER_STEP * N_META, FOURIER_WIDTH), lambda b: (b, 0))
    w_in_specs, w_out_specs, w_out_shapes = zip(*[_cast_specs(w, layer, n_steps) for w in stacked_weights])
    w_in_specs, w_out_specs, w_out_shapes = list(w_in_specs), list(w_out_specs), list(w_out_shapes)
    outs = pl.pallas_call(
        _fourier_kernel,
        out_shape=[jax.ShapeDtypeStruct((n_seq, FOURIER_WIDTH), BF16),
                   jax.ShapeDtypeStruct((n_meta, FOURIER_WIDTH), BF16)] + w_out_shapes,
        grid=(n_steps,),
        in_specs=[per_seq, per_meta] + [_const_spec(t.shape) for t in tables] + w_in_specs,
        out_specs=[per_seq, per_meta] + w_out_specs,
        scratch_shapes=[pltpu.VMEM((half_rows, FOURIER_WIDTH), BF16)] * 5,
        compiler_params=_params(),
        name="fourier",
    )(f[0], f[1], *tables, *stacked_weights)
    return (outs[0], outs[1]), outs[2:]


def _mixout_body(in_refs, const_refs, out_refs):
    yf_ref, ya_ref, h_ref = in_refs
    gain_pre_ref, wg_ref, wf_ref, wa_ref, wo_ref, gain_ref = const_refs
    o_ref, = out_refs
    h = h_ref[...]
    u = _rms(h, gain_pre_ref[...]).astype(BF16)
    g_four = jax.nn.sigmoid(jnp.dot(u, wg_ref[:, :D_MODEL], preferred_element_type=F32))
    mixed = g_four * jnp.dot(yf_ref[...], wf_ref[...], preferred_element_type=F32)
    g_attn = jax.nn.sigmoid(jnp.dot(u, wg_ref[:, D_MODEL:], preferred_element_type=F32))
    mixed = (mixed + g_attn * jnp.dot(ya_ref[...], wa_ref[...], preferred_element_type=F32)).astype(BF16)
    out = jnp.dot(mixed, wo_ref[...], preferred_element_type=F32)
    o_ref[...] = h + _rms(out, gain_ref[...])


def _mixout(yf, ya, h, gain_pre, wg, wf, wa, wo, gain):
    seq_ins = [(yf[0], None), (ya[0], None), (h[0], None)]
    out_seq, out_meta = _token_call(_mixout_body, "mixout", seq_ins, [yf[1], ya[1], h[1]],
                                    [gain_pre, wg, wf, wa, wo, gain], (D_MODEL,), F32)
    return out_seq[0], out_meta[0]


def _ffn_body(in_refs, const_refs, out_refs):
    h_ref, = in_refs
    gain_pre_ref, wg_ref, wu_ref, wd_ref, gain_post_ref = const_refs
    o_ref, = out_refs
    h = h_ref[...]
    u = _rms(h, gain_pre_ref[...]).astype(BF16)
    ff = None
    for lo, hi in FF_CHUNKS:
        gate = jnp.dot(u, wg_ref[:, lo:hi], preferred_element_type=F32)
        up = jnp.dot(u, wu_ref[:, lo:hi], preferred_element_type=F32)
        act = (gate * jax.nn.sigmoid(gate) * up).astype(BF16)
        part = jnp.dot(act, wd_ref[lo:hi, :], preferred_element_type=F32)
        ff = part if ff is None else ff + part
    o_ref[...] = h + _rms(ff, gain_post_ref[...])


def _ffn(h, gain_pre, wg, wu, wd, gain_post):
    out_seq, out_meta = _token_call(_ffn_body, "ffn", [(h[0], None)], [h[1]],
                                    [gain_pre, wg, wu, wd, gain_post], (D_MODEL,), F32)
    return out_seq[0], out_meta[0]


def _rope_tables(positions):
    d = jnp.arange(LANES) % HEAD_DIM
    inv_freq = ROPE_THETA ** (-(2 * (d % (HEAD_DIM // 2))).astype(F32) / HEAD_DIM)
    ang = positions.astype(F32)[:, None] * inv_freq[None, :]
    sign = jnp.where(d < HEAD_DIM // 2, -1.0, 1.0).astype(F32)
    return jnp.cos(ang), jnp.sin(ang) * sign[None, :]


def _pair_heads(w, axis):
    shape = w.shape
    split = shape[:axis] + (N_KV_HEADS, Q_GROUP, HEAD_DIM) + shape[axis + 1:]
    return jnp.swapaxes(w.reshape(split), axis, axis + 1).reshape(shape)


def kernel(x, meta_tokens, w_in, w_fourier_out, w_attn_out, w_o, sink_logits, norm_mix_pre, norm_mix_post,
           norm_ffn_pre, norm_ffn_post, w_ffn_gate, w_ffn_up, w_ffn_down):
    batch, seq, _ = x.shape
    depth = w_in.shape[0]
    assert seq % ROW_TILE == 0 and seq // BLOCK >= 3 and batch % BATCH_PER_STEP == 0

    h = (x.reshape(batch * seq, D_MODEL),
         jnp.broadcast_to(meta_tokens[None].astype(x.dtype), (batch, N_META, D_MODEL)).reshape(-1, D_MODEL))

    rope_seq = _rope_tables(N_META + jnp.arange(seq))
    rope_meta = _rope_tables(jnp.arange(batch * N_META) % N_META)
    fourier_tables = [jnp.asarray(t).astype(BF16) for t in _fourier_tables(seq)]

    gain = lambda t, l: t[l].reshape(1, D_MODEL)

    for l in range(depth):
        gate0 = IN_WIDTH - GATE_WIDTH
        w_l = jnp.concatenate([_pair_heads(w_in[l, :, :ATTN_WIDTH], 1), w_in[l, :, ATTN_WIDTH:gate0]], axis=1)
        (q, k, v, f), (qm, km, vm, fm) = _inproj(h, gain(norm_mix_pre, l), w_l.astype(BF16),
                                                 rope_seq, rope_meta, seq)
        ya, (w_gate, w_four, w_out) = _attention(sink_logits[l], (q, qm), (k, km), (v, vm), seq, l,
                                                 ((w_in, gate0), (w_fourier_out, 0), (w_o, 0)))
        yf, (wg, wu, wd) = _fourier((f, fm), fourier_tables, seq, l, (w_ffn_gate, w_ffn_up, w_ffn_down))
        h = _mixout(yf, ya, h, gain(norm_mix_pre, l), w_gate, w_four,
                    _pair_heads(w_attn_out[l], 0).astype(BF16), w_out, gain(norm_mix_post, l))
        h = _ffn(h, gain(norm_ffn_pre, l), wg, wu, wd, gain(norm_ffn_post, l))

    return h[0].reshape(batch, seq, D_MODEL)
```

```python
import functools

import jax
import jax.numpy as jnp
import numpy as np
from jax import lax
from jax.experimental import pallas as pl
from jax.experimental.pallas import tpu as pltpu

D_MODEL = 1024
N_META = 16
HEAD_DIM = 64
N_HEADS = 8
N_KV_HEADS = 2
Q_GROUP = N_HEADS // N_KV_HEADS
ATTN_WIDTH = N_HEADS * HEAD_DIM
KV_WIDTH = N_KV_HEADS * HEAD_DIM
FOURIER_WIDTH = D_MODEL - ATTN_WIDTH
FOURIER_GROUP = 128
N_FOURIER_GROUPS = FOURIER_WIDTH // FOURIER_GROUP
GATE_WIDTH = 2 * D_MODEL
IN_WIDTH = ATTN_WIDTH + 2 * KV_WIDTH + FOURIER_WIDTH + GATE_WIDTH
PROJ_WIDTHS = (ATTN_WIDTH, KV_WIDTH, KV_WIDTH, FOURIER_WIDTH)
WINDOW = 128
BLOCK = 128
ROPE_THETA = 10000.0
D_FF = 2816
EPS = 1e-6

LANES = 128
SUBLANES = 8
VMEM_LIMIT = 56 * 1024 * 1024
ROW_TILE = 1024
SLAB_ROWS = 512
FF_CHUNKS = ((0, 1536), (1536, D_FF))

BF16 = jnp.bfloat16
F32 = jnp.float32


def _const_spec(shape):
    return pl.BlockSpec(shape, lambda *_: (0,) * len(shape), pipeline_mode=pl.Buffered(1))


def _params():
    return pltpu.CompilerParams(dimension_semantics=("arbitrary",), vmem_limit_bytes=VMEM_LIMIT)


def _rms(x, gain):
    return x * lax.rsqrt(jnp.mean(x * x, axis=-1, keepdims=True) + EPS) * gain


def _cast_specs(stacked, layer, n_steps, col_start=0):
    _, rows, cols = stacked.shape
    assert rows % (n_steps * 2 * SUBLANES) == 0 and col_start % LANES == 0
    block_rows = rows // n_steps
    return (pl.BlockSpec((None, block_rows, cols), lambda b: (layer, b, 0)),
            pl.BlockSpec((block_rows, cols - col_start), lambda b: (b, 0)),
            jax.ShapeDtypeStruct((rows, cols - col_start), BF16))


def _cast_blocks(src_refs, dst_refs):
    for src, dst in zip(src_refs, dst_refs):
        dst[...] = src[:, src.shape[1] - dst.shape[1]:].astype(BF16)


def _token_call(body, name, seq_ins, meta_ins, consts, out_widths, out_dtype, row_tile=ROW_TILE):
    n_seq = seq_ins[0][0].shape[0]
    n_meta = meta_ins[0].shape[0]
    n_tiles = n_seq // row_tile
    n_in, n_const, n_out = len(seq_ins), len(consts), len(out_widths)

    def kernel(*refs):
        seq_refs, refs = refs[:n_in], refs[n_in:]
        meta_refs, refs = refs[:n_in], refs[n_in:]
        const_refs, refs = refs[:n_const], refs[n_const:]
        out_seq, refs = refs[:n_out], refs[n_out:]
        out_meta = refs[:n_out]
        step = pl.program_id(0)

        @pl.when(step < n_tiles)
        def _():
            body(seq_refs, const_refs, out_seq)

        @pl.when(step == n_tiles)
        def _():
            body(meta_refs, const_refs, out_meta)

    last = n_tiles - 1
    seq_specs = []
    for arr, index_fn in seq_ins:
        index_fn = index_fn or (lambda i: i)
        seq_specs.append(pl.BlockSpec((row_tile, arr.shape[1]),
                                      functools.partial(lambda fn, i: (fn(jnp.minimum(i, last)), 0), index_fn)))
    meta_specs = [pl.BlockSpec(arr.shape, lambda i: (0, 0)) for arr in meta_ins]
    const_specs = [_const_spec(c.shape) for c in consts]
    out_shape = ([jax.ShapeDtypeStruct((n_seq, w), out_dtype) for w in out_widths]
                 + [jax.ShapeDtypeStruct((n_meta, w), out_dtype) for w in out_widths])
    out_specs = ([pl.BlockSpec((row_tile, w), lambda i: (jnp.minimum(i, last), 0)) for w in out_widths]
                 + [pl.BlockSpec((n_meta, w), lambda i: (0, 0)) for w in out_widths])
    outs = pl.pallas_call(
        kernel,
        out_shape=out_shape,
        grid=(n_tiles + 1,),
        in_specs=seq_specs + meta_specs + const_specs,
        out_specs=out_specs,
        compiler_params=_params(),
        name=name,
    )(*[a for a, _ in seq_ins], *meta_ins, *consts)
    return outs[:n_out], outs[n_out:]


def _in_slabs(slab_body, slab_rows):
    def body(in_refs, const_refs, out_refs):
        rows = in_refs[0].shape[0]
        slab = slab_rows if rows % slab_rows == 0 else rows
        for start in range(0, rows, slab):
            view = lambda refs: [r.at[pl.ds(start, slab)] for r in refs]
            slab_body(view(in_refs), const_refs, view(out_refs))
    return body


def _inproj_body(in_refs, const_refs, out_refs):
    h_ref, cos_ref, sin_ref = in_refs
    gain_ref, w_ref = const_refs
    q_ref, k_ref, v_ref, f_ref = out_refs
    u = _rms(h_ref[...], gain_ref[...]).astype(BF16)
    cos = cos_ref[...]
    sin = sin_ref[...]
    first_half = (lax.broadcasted_iota(jnp.int32, (1, LANES), 1) % HEAD_DIM) < HEAD_DIM // 2

    def rope(y):
        rot = jnp.where(first_half, pltpu.roll(y, LANES - HEAD_DIM // 2, 1),
                        pltpu.roll(y, HEAD_DIM // 2, 1))
        return y * cos + rot * sin

    def proj(lo, hi):
        return jnp.dot(u, w_ref[:, lo:hi], preferred_element_type=F32)

    q = proj(0, ATTN_WIDTH)
    scale = HEAD_DIM ** -0.5
    for c in range(ATTN_WIDTH // LANES):
        q_ref[:, c * LANES:(c + 1) * LANES] = (rope(q[:, c * LANES:(c + 1) * LANES]) * scale).astype(BF16)
    kv = proj(ATTN_WIDTH, ATTN_WIDTH + 2 * KV_WIDTH)
    k_ref[...] = rope(kv[:, :KV_WIDTH]).astype(BF16)
    v_ref[...] = kv[:, KV_WIDTH:].astype(BF16)
    f0 = ATTN_WIDTH + 2 * KV_WIDTH
    f_ref[...] = proj(f0, f0 + FOURIER_WIDTH).astype(BF16)


def _inproj(h, gain, w, rope_seq, rope_meta, seq):
    table_index = lambda i: 0
    seq_ins = [(h[0], None), (rope_seq[0], table_index), (rope_seq[1], table_index)]
    return _token_call(_in_slabs(_inproj_body, ROW_TILE), "inproj", seq_ins,
                       [h[1], rope_meta[0], rope_meta[1]], [gain, w], PROJ_WIDTHS, BF16, row_tile=seq)


MASKED = -1e30


def _attend(q_rows, k_all, v_ones, bias, sink_ref):
    r = q_rows.shape[0]
    low_lanes = lax.broadcasted_iota(jnp.int32, (1, LANES), 1) < HEAD_DIM
    top_rows = lax.broadcasted_iota(jnp.int32, (2 * r, 1), 0) < r
    bias2 = jnp.concatenate([bias, bias], axis=0)
    zero = jnp.zeros((), BF16)
    outs = []
    for j in range(Q_GROUP):
        qj = q_rows[:, j * LANES:(j + 1) * LANES]
        qm = jnp.concatenate([jnp.where(low_lanes, qj, zero), jnp.where(low_lanes, zero, qj)], axis=0)
        s = lax.dot_general(qm, k_all, (((1,), (1,)), ((), ())), preferred_element_type=F32) + bias2
        sink = jnp.where(top_rows, sink_ref[j], sink_ref[Q_GROUP + j])
        m = jnp.maximum(jnp.max(s, axis=-1, keepdims=True), sink)
        p = jnp.exp((s - m).astype(BF16))
        o = jnp.dot(p, v_ones, preferred_element_type=F32)
        o = o[:, :LANES] / (o[:, LANES:] + jnp.exp(sink - m))
        outs.append(jnp.where(low_lanes, o[:r], o[r:]))
    return jnp.concatenate(outs, axis=1)


def _attn_kernel(sink_ref, q_ref, k_ref, v_ref, qm_ref, km_ref, vm_ref, bias_ref, bias_meta_ref,
                 wa_ref, wb_ref, wc_ref, o_ref, om_ref, wa_out, wb_out, wc_out):
    _cast_blocks((wa_ref, wb_ref, wc_ref), (wa_out, wb_out, wc_out))
    seq = q_ref.shape[0]
    n_blocks = seq // BLOCK
    band = 3 * BLOCK

    def with_ones(v):
        return jnp.concatenate([v, jnp.ones_like(v)], axis=1)

    n_first = bias_meta_ref.shape[1] - N_META
    k_first = jnp.concatenate([km_ref[...], k_ref[0:n_first, :]], axis=0)
    v_first = jnp.concatenate([vm_ref[...], v_ref[0:n_first, :]], axis=0)
    om_ref[...] = _attend(qm_ref[...], k_first, with_ones(v_first), bias_meta_ref[...], sink_ref).astype(BF16)

    k_head = k_first[0:BLOCK]
    v_head = v_first[0:BLOCK]

    def body(i, carry):
        q_start = pl.multiple_of(BLOCK * i, BLOCK)
        kb = jnp.clip(i - 1, 0, n_blocks - 3)
        k_start = pl.multiple_of(BLOCK * kb, BLOCK)
        kind = jnp.where(i == 0, 0, jnp.where(i == n_blocks - 1, 2, 1))
        k_all = jnp.concatenate([k_head, k_ref[pl.ds(k_start, band), :]], axis=0)
        v_all = jnp.concatenate([v_head, v_ref[pl.ds(k_start, band), :]], axis=0)
        o = _attend(q_ref[pl.ds(q_start, BLOCK), :], k_all, with_ones(v_all), bias_ref[kind], sink_ref)
        o_ref[pl.ds(q_start, BLOCK), :] = o.astype(BF16)
        return carry

    lax.fori_loop(0, n_blocks, body, 0, unroll=16)


def _attention_bias():
    row = np.arange(BLOCK)[:, None]
    col = np.arange(BLOCK + 3 * BLOCK)[None, :]
    kinds = []
    for first_band_block in (0, -1, -2):
        rel = (col - BLOCK) + BLOCK * first_band_block - row
        kinds.append((col < N_META) | ((col >= BLOCK) & (np.abs(rel) <= WINDOW)))
    p_idx = np.arange(N_META)[:, None]
    col = np.arange(2 * BLOCK)[None, :]
    meta = (col < N_META) | (col - p_idx <= WINDOW)
    to_bias = lambda visible: np.where(visible, 0.0, MASKED).astype(np.float32)
    return to_bias(np.stack(kinds)), to_bias(meta)


def _attention(sink, q, k, v, seq, layer, cast_jobs):
    n_seq, n_meta = q[0].shape[0], q[1].shape[0]
    n_steps = n_seq // seq
    bias, bias_meta = _attention_bias()
    per_seq = lambda width: pl.BlockSpec((seq, width), lambda b: (b, 0))
    per_meta = lambda width: pl.BlockSpec((N_META, width), lambda b: (b, 0))
    widths = (ATTN_WIDTH, KV_WIDTH, KV_WIDTH)
    w_in_specs, w_out_specs, w_out_shapes = zip(*[_cast_specs(w, layer, n_steps, c0) for w, c0 in cast_jobs])
    outs = pl.pallas_call(
        _attn_kernel,
        out_shape=[jax.ShapeDtypeStruct((n_seq, ATTN_WIDTH), BF16),
                   jax.ShapeDtypeStruct((n_meta, ATTN_WIDTH), BF16)] + list(w_out_shapes),
        grid=(n_steps,),
        in_specs=([pl.BlockSpec(memory_space=pltpu.SMEM)] + [per_seq(w) for w in widths]
                  + [per_meta(w) for w in widths] + [_const_spec(bias.shape), _const_spec(bias_meta.shape)]
                  + list(w_in_specs)),
        out_specs=[per_seq(ATTN_WIDTH), per_meta(ATTN_WIDTH)] + list(w_out_specs),
        compiler_params=_params(),
        name="attn",
    )(sink, q[0], k[0], v[0], q[1], k[1], v[1], bias, bias_meta, *[w for w, _ in cast_jobs])
    return (outs[0], outs[1]), outs[2:]


XCH = 128
XCH_WIN = XCH + N_META


def _fourier_kernel(f_ref, fm_ref, xch_ref, xch_meta_ref, chan_c_ref, chan_s_ref, cos_ref, sin_ref,
                    wa_ref, wb_ref, wc_ref, y_ref, ym_ref, wa_out, wb_out, wc_out,
                    ex_ref, ox_ref, ec_ref, os_ref, t_ref):
    _cast_blocks((wa_ref, wb_ref, wc_ref), (wa_out, wb_out, wc_out))

    seq = f_ref.shape[0]
    half = seq // 2
    centre = half - N_META // 2
    f32_dot = functools.partial(jnp.dot, preferred_element_type=F32)

    fm = fm_ref[...].astype(F32)
    gm = f32_dot(xch_meta_ref[...], f_ref[seq - N_META:seq, :])
    ex_ref[0:N_META, :] = (fm + gm).astype(BF16)
    ox_ref[0:N_META, :] = (fm - gm).astype(BF16)
    for c in range(half // XCH):
        w = seq - XCH_WIN - XCH * c
        g = f32_dot(xch_ref[...], f_ref[w:w + XCH_WIN, :])
        fs = f_ref[XCH * c:XCH * (c + 1), :].astype(F32)
        if XCH * (c + 1) > centre:
            paired = XCH * c + lax.broadcasted_iota(jnp.int32, (XCH, 1), 0) < centre
            e = fs + jnp.where(paired, g, 0.0)
            o = jnp.where(paired, fs - g, 0.0)
        else:
            e, o = fs + g, fs - g
        rows = slice(N_META + XCH * c, N_META + XCH * (c + 1))
        ex_ref[rows, :] = e.astype(BF16)
        ox_ref[rows, :] = o.astype(BF16)

    for p in range(FOURIER_WIDTH // (2 * FOURIER_GROUP)):
        lanes = slice(2 * FOURIER_GROUP * p, 2 * FOURIER_GROUP * (p + 1))
        ec_ref[:, lanes] = f32_dot(ex_ref[:, lanes], chan_c_ref[...]).astype(BF16)
        os_ref[:, lanes] = f32_dot(ox_ref[:, lanes], chan_s_ref[...]).astype(BF16)

    a = f32_dot(cos_ref[...], ec_ref[...])
    b = f32_dot(sin_ref[...], os_ref[...])
    u = a - b
    t_ref[...] = (a + b).astype(BF16)
    ym_ref[...] = u[0:N_META].astype(BF16)
    y_ref[0:half - N_META, :] = u[N_META:half].astype(BF16)
    mirrored = f32_dot(xch_meta_ref[...], t_ref[half:half + N_META, :])
    low = lax.broadcasted_iota(jnp.int32, (N_META, 1), 0) < N_META // 2
    y_ref[half - N_META:half, :] = jnp.where(low, u[half:half + N_META], mirrored).astype(BF16)
    for c in range(half // XCH):
        start = half + XCH * c
        w = seq - start - XCH
        y_ref[start:start + XCH, :] = f32_dot(xch_ref[...], t_ref[w:w + XCH_WIN, :]).astype(BF16)


def _fourier_tables(seq):
    n = N_META + seq
    h = n // 2
    rows = N_META + seq // 2
    idx = np.arange(rows, dtype=np.int64)
    ang = ((idx[:, None] * idx[None, :]) % n) * (2.0 * np.pi / n)
    inside = idx <= h
    strict = (idx >= 1) & (idx < h)
    cos_t = np.where(inside[:, None] & inside[None, :], np.cos(ang), 0.0) * n ** -0.5
    sin_t = np.where(strict[:, None] & strict[None, :], np.sin(ang), 0.0) * n ** -0.5
    d = np.arange(FOURIER_GROUP, dtype=np.int64)
    ang = ((d[:, None] * d[None, :]) % FOURIER_GROUP) * (2.0 * np.pi / FOURIER_GROUP)
    pair = np.eye(2)
    chan_c = np.kron(pair, np.cos(ang)) * FOURIER_GROUP ** -0.5
    chan_s = np.kron(pair, np.sin(ang)) * FOURIER_GROUP ** -0.5
    xch = np.zeros((XCH, XCH_WIN))
    xch[np.arange(XCH), XCH - np.arange(XCH)] = 1.0
    xch_meta = np.zeros((N_META, N_META))
    xch_meta[np.arange(1, N_META), N_META - np.arange(1, N_META)] = 1.0
    return [t.astype(np.float32) for t in (xch, xch_meta, chan_c, chan_s, cos_t, sin_t)]


def _fourier(f, tables, seq, layer, stacked_weights):
    n_seq, n_meta = f[0].shape[0], f[1].shape[0]
    n_steps = n_seq // seq
    assert seq % (2 * XCH) == 0
    half_rows = N_META + seq // 2
    per_seq = pl.BlockSpec((seq, FOURIER_WIDTH), lambda b: (b, 0))
    per_meta = pl.BlockSpec((N_META, FOURIER_WIDTH), lambda b: (b, 0))
    w_in_specs, w_out_specs, w_out_shapes = zip(*[_cast_specs(w, layer, n_steps) for w in stacked_weights])
    w_in_specs, w_out_specs, w_out_shapes = list(w_in_specs), list(w_out_specs), list(w_out_shapes)
    outs = pl.pallas_call(
        _fourier_kernel,
        out_shape=[jax.ShapeDtypeStruct((n_seq, FOURIER_WIDTH), BF16),
                   jax.ShapeDtypeStruct((n_meta, FOURIER_WIDTH), BF16)] + w_out_shapes,
        grid=(n_steps,),
        in_specs=[per_seq, per_meta] + [_const_spec(t.shape) for t in tables] + w_in_specs,
        out_specs=[per_seq, per_meta] + w_out_specs,
        scratch_shapes=[pltpu.VMEM((half_rows, FOURIER_WIDTH), BF16)] * 5,
        compiler_params=_params(),
        name="fourier",
    )(f[0], f[1], *tables, *stacked_weights)
    return (outs[0], outs[1]), outs[2:]


def _mixout_body(in_refs, const_refs, out_refs):
    yf_ref, ya_ref, h_ref = in_refs
    gain_pre_ref, wg_ref, wf_ref, wa_ref, wo_ref, gain_ref = const_refs
    o_ref, = out_refs
    h = h_ref[...]
    u = _rms(h, gain_pre_ref[...]).astype(BF16)
    g_four = jax.nn.sigmoid(jnp.dot(u, wg_ref[:, :D_MODEL], preferred_element_type=F32))
    mixed = g_four * jnp.dot(yf_ref[...], wf_ref[...], preferred_element_type=F32)
    g_attn = jax.nn.sigmoid(jnp.dot(u, wg_ref[:, D_MODEL:], preferred_element_type=F32))
    mixed = (mixed + g_attn * jnp.dot(ya_ref[...], wa_ref[...], preferred_element_type=F32)).astype(BF16)
    out = jnp.dot(mixed, wo_ref[...], preferred_element_type=F32)
    o_ref[...] = h + _rms(out, gain_ref[...])


def _mixout(yf, ya, h, gain_pre, wg, wf, wa, wo, gain):
    seq_ins = [(yf[0], None), (ya[0], None), (h[0], None)]
    out_seq, out_meta = _token_call(_in_slabs(_mixout_body, SLAB_ROWS), "mixout", seq_ins, [yf[1], ya[1], h[1]],
                                    [gain_pre, wg, wf, wa, wo, gain], (D_MODEL,), F32)
    return out_seq[0], out_meta[0]


def _ffn_body(in_refs, const_refs, out_refs):
    h_ref, = in_refs
    gain_pre_ref, wg_ref, wu_ref, wd_ref, gain_post_ref = const_refs
    o_ref, = out_refs
    h = h_ref[...]
    u = _rms(h, gain_pre_ref[...]).astype(BF16)
    ff = None
    for lo, hi in FF_CHUNKS:
        gate = jnp.dot(u, wg_ref[:, lo:hi], preferred_element_type=F32)
        up = jnp.dot(u, wu_ref[:, lo:hi], preferred_element_type=F32)
        act = (gate * jax.nn.sigmoid(gate) * up).astype(BF16)
        part = jnp.dot(act, wd_ref[lo:hi, :], preferred_element_type=F32)
        ff = part if ff is None else ff + part
    o_ref[...] = h + _rms(ff, gain_post_ref[...])


def _ffn(h, gain_pre, wg, wu, wd, gain_post):
    out_seq, out_meta = _token_call(_in_slabs(_ffn_body, SLAB_ROWS), "ffn", [(h[0], None)], [h[1]],
                                    [gain_pre, wg, wu, wd, gain_post], (D_MODEL,), F32)
    return out_seq[0], out_meta[0]


def _rope_tables(positions):
    d = jnp.arange(LANES) % HEAD_DIM
    inv_freq = ROPE_THETA ** (-(2 * (d % (HEAD_DIM // 2))).astype(F32) / HEAD_DIM)
    ang = positions.astype(F32)[:, None] * inv_freq[None, :]
    sign = jnp.where(d < HEAD_DIM // 2, -1.0, 1.0).astype(F32)
    return jnp.cos(ang), jnp.sin(ang) * sign[None, :]


def _pair_heads(w, axis):
    shape = w.shape
    split = shape[:axis] + (N_KV_HEADS, Q_GROUP, HEAD_DIM) + shape[axis + 1:]
    return jnp.swapaxes(w.reshape(split), axis, axis + 1).reshape(shape)


def kernel(x, meta_tokens, w_in, w_fourier_out, w_attn_out, w_o, sink_logits, norm_mix_pre, norm_mix_post,
           norm_ffn_pre, norm_ffn_post, w_ffn_gate, w_ffn_up, w_ffn_down):
    batch, seq, _ = x.shape
    depth = w_in.shape[0]
    assert seq % ROW_TILE == 0 and seq // BLOCK >= 3

    h = (x.reshape(batch * seq, D_MODEL),
         jnp.broadcast_to(meta_tokens[None].astype(x.dtype), (batch, N_META, D_MODEL)).reshape(-1, D_MODEL))

    rope_seq = _rope_tables(N_META + jnp.arange(seq))
    rope_meta = _rope_tables(jnp.arange(batch * N_META) % N_META)
    fourier_tables = [jnp.asarray(t).astype(BF16) for t in _fourier_tables(seq)]

    gain = lambda t, l: t[l].reshape(1, D_MODEL)

    for l in range(depth):
        gate0 = IN_WIDTH - GATE_WIDTH
        w_l = jnp.concatenate([_pair_heads(w_in[l, :, :ATTN_WIDTH], 1), w_in[l, :, ATTN_WIDTH:gate0]], axis=1)
        (q, k, v, f), (qm, km, vm, fm) = _inproj(h, gain(norm_mix_pre, l), w_l.astype(BF16),
                                                 rope_seq, rope_meta, seq)
        ya, (w_gate, w_four, w_out) = _attention(sink_logits[l], (q, qm), (k, km), (v, vm), seq, l,
                                                 ((w_in, gate0), (w_fourier_out, 0), (w_o, 0)))
        yf, (wg, wu, wd) = _fourier((f, fm), fourier_tables, seq, l, (w_ffn_gate, w_ffn_up, w_ffn_down))
        h = _mixout(yf, ya, h, gain(norm_mix_pre, l), w_gate, w_four,
                    _pair_heads(w_attn_out[l], 0).astype(BF16), w_out, gain(norm_mix_post, l))
        h = _ffn(h, gain(norm_ffn_pre, l), wg, wu, wd, gain(norm_ffn_post, l))

    return h[0].reshape(batch, seq, D_MODEL)
```

```python
import functools

import jax
import jax.numpy as jnp
import numpy as np
from jax import lax
from jax.experimental import pallas as pl
from jax.experimental.pallas import tpu as pltpu

D_MODEL = 1024
N_META = 16
HEAD_DIM = 64
N_HEADS = 8
N_KV_HEADS = 2
Q_GROUP = N_HEADS // N_KV_HEADS
ATTN_WIDTH = N_HEADS * HEAD_DIM
KV_WIDTH = N_KV_HEADS * HEAD_DIM
FOURIER_WIDTH = D_MODEL - ATTN_WIDTH
FOURIER_GROUP = 128
N_FOURIER_GROUPS = FOURIER_WIDTH // FOURIER_GROUP
GATE_WIDTH = 2 * D_MODEL
IN_WIDTH = ATTN_WIDTH + 2 * KV_WIDTH + FOURIER_WIDTH + GATE_WIDTH
PROJ_WIDTHS = (ATTN_WIDTH, KV_WIDTH, KV_WIDTH, FOURIER_WIDTH)
WINDOW = 128
BLOCK = 128
ROPE_THETA = 10000.0
D_FF = 2816
EPS = 1e-6

LANES = 128
SUBLANES = 8
VMEM_LIMIT = 56 * 1024 * 1024
ROW_TILE = 1024
FF_CHUNKS = ((0, 1536), (1536, D_FF))

BF16 = jnp.bfloat16
F32 = jnp.float32


def _const_spec(shape):
    return pl.BlockSpec(shape, lambda *_: (0,) * len(shape), pipeline_mode=pl.Buffered(1))


def _params():
    return pltpu.CompilerParams(dimension_semantics=("arbitrary",), vmem_limit_bytes=VMEM_LIMIT)


def _rms(x, gain):
    return x * lax.rsqrt(jnp.mean(x * x, axis=-1, keepdims=True) + EPS) * gain


def _cast_specs(stacked, layer, n_steps, col_start=0):
    _, rows, cols = stacked.shape
    assert rows % (n_steps * 2 * SUBLANES) == 0 and col_start % LANES == 0
    block_rows = rows // n_steps
    return (pl.BlockSpec((None, block_rows, cols), lambda b: (layer, b, 0)),
            pl.BlockSpec((block_rows, cols - col_start), lambda b: (b, 0)),
            jax.ShapeDtypeStruct((rows, cols - col_start), BF16))


def _cast_blocks(src_refs, dst_refs):
    for src, dst in zip(src_refs, dst_refs):
        dst[...] = src[:, src.shape[1] - dst.shape[1]:].astype(BF16)


def _token_call(body, name, seq_ins, meta_ins, consts, out_widths, out_dtype, row_tile=ROW_TILE):
    n_seq = seq_ins[0][0].shape[0]
    n_meta = meta_ins[0].shape[0]
    n_tiles = n_seq // row_tile
    n_in, n_const, n_out = len(seq_ins), len(consts), len(out_widths)

    def kernel(*refs):
        seq_refs, refs = refs[:n_in], refs[n_in:]
        meta_refs, refs = refs[:n_in], refs[n_in:]
        const_refs, refs = refs[:n_const], refs[n_const:]
        out_seq, refs = refs[:n_out], refs[n_out:]
        out_meta = refs[:n_out]
        step = pl.program_id(0)

        @pl.when(step < n_tiles)
        def _():
            body(seq_refs, const_refs, out_seq)

        @pl.when(step == n_tiles)
        def _():
            body(meta_refs, const_refs, out_meta)

    last = n_tiles - 1
    seq_specs = []
    for arr, index_fn in seq_ins:
        index_fn = index_fn or (lambda i: i)
        seq_specs.append(pl.BlockSpec((row_tile, arr.shape[1]),
                                      functools.partial(lambda fn, i: (fn(jnp.minimum(i, last)), 0), index_fn)))
    meta_specs = [pl.BlockSpec(arr.shape, lambda i: (0, 0)) for arr in meta_ins]
    const_specs = [_const_spec(c.shape) for c in consts]
    out_shape = ([jax.ShapeDtypeStruct((n_seq, w), out_dtype) for w in out_widths]
                 + [jax.ShapeDtypeStruct((n_meta, w), out_dtype) for w in out_widths])
    out_specs = ([pl.BlockSpec((row_tile, w), lambda i: (jnp.minimum(i, last), 0)) for w in out_widths]
                 + [pl.BlockSpec((n_meta, w), lambda i: (0, 0)) for w in out_widths])
    outs = pl.pallas_call(
        kernel,
        out_shape=out_shape,
        grid=(n_tiles + 1,),
        in_specs=seq_specs + meta_specs + const_specs,
        out_specs=out_specs,
        compiler_params=_params(),
        name=name,
    )(*[a for a, _ in seq_ins], *meta_ins, *consts)
    return outs[:n_out], outs[n_out:]


def _in_slabs(slab_body, slab_rows):
    def body(in_refs, const_refs, out_refs):
        rows = in_refs[0].shape[0]
        slab = slab_rows if rows % slab_rows == 0 else rows
        for start in range(0, rows, slab):
            view = lambda refs: [r.at[pl.ds(start, slab)] for r in refs]
            slab_body(view(in_refs), const_refs, view(out_refs))
    return body


def _inproj_body(in_refs, const_refs, out_refs):
    h_ref, cos_ref, sin_ref = in_refs
    gain_ref, w_ref = const_refs
    q_ref, k_ref, v_ref, f_ref = out_refs
    u = _rms(h_ref[...], gain_ref[...]).astype(BF16)
    cos = cos_ref[...]
    sin = sin_ref[...]
    first_half = (lax.broadcasted_iota(jnp.int32, (1, LANES), 1) % HEAD_DIM) < HEAD_DIM // 2

    def rope(y):
        rot = jnp.where(first_half, pltpu.roll(y, LANES - HEAD_DIM // 2, 1),
                        pltpu.roll(y, HEAD_DIM // 2, 1))
        return y * cos + rot * sin

    def proj(lo, hi):
        return jnp.dot(u, w_ref[:, lo:hi], preferred_element_type=F32)

    q = proj(0, ATTN_WIDTH)
    scale = HEAD_DIM ** -0.5
    for c in range(ATTN_WIDTH // LANES):
        q_ref[:, c * LANES:(c + 1) * LANES] = (rope(q[:, c * LANES:(c + 1) * LANES]) * scale).astype(BF16)
    kv = proj(ATTN_WIDTH, ATTN_WIDTH + 2 * KV_WIDTH)
    k_ref[...] = rope(kv[:, :KV_WIDTH]).astype(BF16)
    v_ref[...] = kv[:, KV_WIDTH:].astype(BF16)
    f0 = ATTN_WIDTH + 2 * KV_WIDTH
    f_ref[...] = proj(f0, f0 + FOURIER_WIDTH).astype(BF16)


def _inproj(h, gain, w, rope_seq, rope_meta, seq):
    table_index = lambda i: 0
    seq_ins = [(h[0], None), (rope_seq[0], table_index), (rope_seq[1], table_index)]
    return _token_call(_in_slabs(_inproj_body, ROW_TILE), "inproj", seq_ins,
                       [h[1], rope_meta[0], rope_meta[1]], [gain, w], PROJ_WIDTHS, BF16, row_tile=seq)


MASKED = -1e30


def _attend(q_rows, k_all, v_ones, bias, sink_ref):
    r = q_rows.shape[0]
    low_lanes = lax.broadcasted_iota(jnp.int32, (1, LANES), 1) < HEAD_DIM
    top_rows = lax.broadcasted_iota(jnp.int32, (2 * r, 1), 0) < r
    bias2 = jnp.concatenate([bias, bias], axis=0)
    zero = jnp.zeros((), BF16)
    outs = []
    for j in range(Q_GROUP):
        qj = q_rows[:, j * LANES:(j + 1) * LANES]
        qm = jnp.concatenate([jnp.where(low_lanes, qj, zero), jnp.where(low_lanes, zero, qj)], axis=0)
        s = lax.dot_general(qm, k_all, (((1,), (1,)), ((), ())), preferred_element_type=F32) + bias2
        sink = jnp.where(top_rows, sink_ref[j], sink_ref[Q_GROUP + j])
        m = jnp.maximum(jnp.max(s, axis=-1, keepdims=True), sink)
        p = jnp.exp((s - m).astype(BF16))
        o = jnp.dot(p, v_ones, preferred_element_type=F32)
        o = o[:, :LANES] / (o[:, LANES:] + jnp.exp(sink - m))
        outs.append(jnp.where(low_lanes, o[:r], o[r:]))
    return jnp.concatenate(outs, axis=1)


def _attn_kernel(sink_ref, q_ref, k_ref, v_ref, qm_ref, km_ref, vm_ref, bias_ref, bias_meta_ref,
                 wa_ref, wb_ref, wc_ref, o_ref, om_ref, wa_out, wb_out, wc_out):
    _cast_blocks((wa_ref, wb_ref, wc_ref), (wa_out, wb_out, wc_out))
    seq = q_ref.shape[0]
    n_blocks = seq // BLOCK
    band = 3 * BLOCK

    def with_ones(v):
        return jnp.concatenate([v, jnp.ones_like(v)], axis=1)

    n_first = bias_meta_ref.shape[1] - N_META
    k_first = jnp.concatenate([km_ref[...], k_ref[0:n_first, :]], axis=0)
    v_first = jnp.concatenate([vm_ref[...], v_ref[0:n_first, :]], axis=0)
    om_ref[...] = _attend(qm_ref[...], k_first, with_ones(v_first), bias_meta_ref[...], sink_ref).astype(BF16)

    k_head = k_first[0:BLOCK]
    v_head = v_first[0:BLOCK]

    def body(i, carry):
        q_start = pl.multiple_of(BLOCK * i, BLOCK)
        kb = jnp.clip(i - 1, 0, n_blocks - 3)
        k_start = pl.multiple_of(BLOCK * kb, BLOCK)
        kind = jnp.where(i == 0, 0, jnp.where(i == n_blocks - 1, 2, 1))
        k_all = jnp.concatenate([k_head, k_ref[pl.ds(k_start, band), :]], axis=0)
        v_all = jnp.concatenate([v_head, v_ref[pl.ds(k_start, band), :]], axis=0)
        o = _attend(q_ref[pl.ds(q_start, BLOCK), :], k_all, with_ones(v_all), bias_ref[kind], sink_ref)
        o_ref[pl.ds(q_start, BLOCK), :] = o.astype(BF16)
        return carry

    lax.fori_loop(0, n_blocks, body, 0, unroll=16)


def _attention_bias():
    row = np.arange(BLOCK)[:, None]
    col = np.arange(BLOCK + 3 * BLOCK)[None, :]
    kinds = []
    for first_band_block in (0, -1, -2):
        rel = (col - BLOCK) + BLOCK * first_band_block - row
        kinds.append((col < N_META) | ((col >= BLOCK) & (np.abs(rel) <= WINDOW)))
    p_idx = np.arange(N_META)[:, None]
    col = np.arange(2 * BLOCK)[None, :]
    meta = (col < N_META) | (col - p_idx <= WINDOW)
    to_bias = lambda visible: np.where(visible, 0.0, MASKED).astype(np.float32)
    return to_bias(np.stack(kinds)), to_bias(meta)


def _attention(sink, q, k, v, seq, layer, cast_jobs):
    n_seq, n_meta = q[0].shape[0], q[1].shape[0]
    n_steps = n_seq // seq
    bias, bias_meta = _attention_bias()
    per_seq = lambda width: pl.BlockSpec((seq, width), lambda b: (b, 0))
    per_meta = lambda width: pl.BlockSpec((N_META, width), lambda b: (b, 0))
    widths = (ATTN_WIDTH, KV_WIDTH, KV_WIDTH)
    w_in_specs, w_out_specs, w_out_shapes = zip(*[_cast_specs(w, layer, n_steps, c0) for w, c0 in cast_jobs])
    outs = pl.pallas_call(
        _attn_kernel,
        out_shape=[jax.ShapeDtypeStruct((n_seq, ATTN_WIDTH), BF16),
                   jax.ShapeDtypeStruct((n_meta, ATTN_WIDTH), BF16)] + list(w_out_shapes),
        grid=(n_steps,),
        in_specs=([pl.BlockSpec(memory_space=pltpu.SMEM)] + [per_seq(w) for w in widths]
                  + [per_meta(w) for w in widths] + [_const_spec(bias.shape), _const_spec(bias_meta.shape)]
                  + list(w_in_specs)),
        out_specs=[per_seq(ATTN_WIDTH), per_meta(ATTN_WIDTH)] + list(w_out_specs),
        compiler_params=_params(),
        name="attn",
    )(sink, q[0], k[0], v[0], q[1], k[1], v[1], bias, bias_meta, *[w for w, _ in cast_jobs])
    return (outs[0], outs[1]), outs[2:]


XCH = 128
XCH_WIN = XCH + N_META


def _fourier_kernel(f_ref, fm_ref, xch_ref, xch_meta_ref, chan_c_ref, chan_s_ref, cos_ref, sin_ref,
                    wa_ref, wb_ref, wc_ref, y_ref, ym_ref, wa_out, wb_out, wc_out,
                    ex_ref, ox_ref, ec_ref, os_ref, t_ref):
    _cast_blocks((wa_ref, wb_ref, wc_ref), (wa_out, wb_out, wc_out))

    seq = f_ref.shape[0]
    half = seq // 2
    centre = half - N_META // 2
    f32_dot = functools.partial(jnp.dot, preferred_element_type=F32)

    fm = fm_ref[...].astype(F32)
    gm = f32_dot(xch_meta_ref[...], f_ref[seq - N_META:seq, :])
    ex_ref[0:N_META, :] = (fm + gm).astype(BF16)
    ox_ref[0:N_META, :] = (fm - gm).astype(BF16)
    for c in range(half // XCH):
        w = seq - XCH_WIN - XCH * c
        g = f32_dot(xch_ref[...], f_ref[w:w + XCH_WIN, :])
        fs = f_ref[XCH * c:XCH * (c + 1), :].astype(F32)
        if XCH * (c + 1) > centre:
            paired = XCH * c + lax.broadcasted_iota(jnp.int32, (XCH, 1), 0) < centre
            e = fs + jnp.where(paired, g, 0.0)
            o = jnp.where(paired, fs - g, 0.0)
        else:
            e, o = fs + g, fs - g
        rows = slice(N_META + XCH * c, N_META + XCH * (c + 1))
        ex_ref[rows, :] = e.astype(BF16)
        ox_ref[rows, :] = o.astype(BF16)

    for p in range(FOURIER_WIDTH // (2 * FOURIER_GROUP)):
        lanes = slice(2 * FOURIER_GROUP * p, 2 * FOURIER_GROUP * (p + 1))
        ec_ref[:, lanes] = f32_dot(ex_ref[:, lanes], chan_c_ref[...]).astype(BF16)
        os_ref[:, lanes] = f32_dot(ox_ref[:, lanes], chan_s_ref[...]).astype(BF16)

    a = f32_dot(cos_ref[...], ec_ref[...])
    b = f32_dot(sin_ref[...], os_ref[...])
    u = a - b
    t_ref[...] = (a + b).astype(BF16)
    ym_ref[...] = u[0:N_META].astype(BF16)
    y_ref[0:half - N_META, :] = u[N_META:half].astype(BF16)
    mirrored = f32_dot(xch_meta_ref[...], t_ref[half:half + N_META, :])
    low = lax.broadcasted_iota(jnp.int32, (N_META, 1), 0) < N_META // 2
    y_ref[half - N_META:half, :] = jnp.where(low, u[half:half + N_META], mirrored).astype(BF16)
    for c in range(half // XCH):
        start = half + XCH * c
        w = seq - start - XCH
        y_ref[start:start + XCH, :] = f32_dot(xch_ref[...], t_ref[w:w + XCH_WIN, :]).astype(BF16)


def _fourier_tables(seq):
    n = N_META + seq
    h = n // 2
    rows = N_META + seq // 2
    idx = np.arange(rows, dtype=np.int64)
    ang = ((idx[:, None] * idx[None, :]) % n) * (2.0 * np.pi / n)
    inside = idx <= h
    strict = (idx >= 1) & (idx < h)
    cos_t = np.where(inside[:, None] & inside[None, :], np.cos(ang), 0.0) * n ** -0.5
    sin_t = np.where(strict[:, None] & strict[None, :], np.sin(ang), 0.0) * n ** -0.5
    d = np.arange(FOURIER_GROUP, dtype=np.int64)
    ang = ((d[:, None] * d[None, :]) % FOURIER_GROUP) * (2.0 * np.pi / FOURIER_GROUP)
    pair = np.eye(2)
    chan_c = np.kron(pair, np.cos(ang)) * FOURIER_GROUP ** -0.5
    chan_s = np.kron(pair, np.sin(ang)) * FOURIER_GROUP ** -0.5
    xch = np.zeros((XCH, XCH_WIN))
    xch[np.arange(XCH), XCH - np.arange(XCH)] = 1.0
    xch_meta = np.zeros((N_META, N_META))
    xch_meta[np.arange(1, N_META), N_META - np.arange(1, N_META)] = 1.0
    return [t.astype(np.float32) for t in (xch, xch_meta, chan_c, chan_s, cos_t, sin_t)]


def _fourier(f, tables, seq, layer, stacked_weights):
    n_seq, n_meta = f[0].shape[0], f[1].shape[0]
    n_steps = n_seq // seq
    assert seq % (2 * XCH) == 0
    half_rows = N_META + seq // 2
    per_seq = pl.BlockSpec((seq, FOURIER_WIDTH), lambda b: (b, 0))
    per_meta = pl.BlockSpec((N_META, FOURIER_WIDTH), lambda b: (b, 0))
    w_in_specs, w_out_specs, w_out_shapes = zip(*[_cast_specs(w, layer, n_steps) for w in stacked_weights])
    w_in_specs, w_out_specs, w_out_shapes = list(w_in_specs), list(w_out_specs), list(w_out_shapes)
    outs = pl.pallas_call(
        _fourier_kernel,
        out_shape=[jax.ShapeDtypeStruct((n_seq, FOURIER_WIDTH), BF16),
                   jax.ShapeDtypeStruct((n_meta, FOURIER_WIDTH), BF16)] + w_out_shapes,
        grid=(n_steps,),
        in_specs=[per_seq, per_meta] + [_const_spec(t.shape) for t in tables] + w_in_specs,
        out_specs=[per_seq, per_meta] + w_out_specs,
        scratch_shapes=[pltpu.VMEM((half_rows, FOURIER_WIDTH), BF16)] * 5,
        compiler_params=_params(),
        name="fourier",
    )(f[0], f[1], *tables, *stacked_weights)
    return (outs[0], outs[1]), outs[2:]


def _mixout_body(in_refs, const_refs, out_refs):
    yf_ref, ya_ref, h_ref = in_refs
    gain_pre_ref, wg_ref, wf_ref, wa_ref, wo_ref, gain_ref = const_refs
    o_ref, = out_refs
    h = h_ref[...]
    u = _rms(h, gain_pre_ref[...]).astype(BF16)
    g_four = jax.nn.sigmoid(jnp.dot(u, wg_ref[:, :D_MODEL], preferred_element_type=F32))
    mixed = g_four * jnp.dot(yf_ref[...], wf_ref[...], preferred_element_type=F32)
    g_attn = jax.nn.sigmoid(jnp.dot(u, wg_ref[:, D_MODEL:], preferred_element_type=F32))
    mixed = (mixed + g_attn * jnp.dot(ya_ref[...], wa_ref[...], preferred_element_type=F32)).astype(BF16)
    out = jnp.dot(mixed, wo_ref[...], preferred_element_type=F32)
    o_ref[...] = h + _rms(out, gain_ref[...])


def _mixout(yf, ya, h, gain_pre, wg, wf, wa, wo, gain):
    seq_ins = [(yf[0], None), (ya[0], None), (h[0], None)]
    out_seq, out_meta = _token_call(_mixout_body, "mixout", seq_ins, [yf[1], ya[1], h[1]],
                                    [gain_pre, wg, wf, wa, wo, gain], (D_MODEL,), F32)
    return out_seq[0], out_meta[0]


def _ffn_body(in_refs, const_refs, out_refs):
    h_ref, = in_refs
    gain_pre_ref, wg_ref, wu_ref, wd_ref, gain_post_ref = const_refs
    o_ref, = out_refs
    h = h_ref[...]
    u = _rms(h, gain_pre_ref[...]).astype(BF16)
    ff = None
    for lo, hi in FF_CHUNKS:
        gate = jnp.dot(u, wg_ref[:, lo:hi], preferred_element_type=F32)
        up = jnp.dot(u, wu_ref[:, lo:hi], preferred_element_type=F32)
        act = (gate * jax.nn.sigmoid(gate) * up).astype(BF16)
        part = jnp.dot(act, wd_ref[lo:hi, :], preferred_element_type=F32)
        ff = part if ff is None else ff + part
    o_ref[...] = h + _rms(ff, gain_post_ref[...])


def _ffn(h, gain_pre, wg, wu, wd, gain_post):
    out_seq, out_meta = _token_call(_ffn_body, "ffn", [(h[0], None)], [h[1]],
                                    [gain_pre, wg, wu, wd, gain_post], (D_MODEL,), F32)
    return out_seq[0], out_meta[0]


def _rope_tables(positions):
    d = jnp.arange(LANES) % HEAD_DIM
    inv_freq = ROPE_THETA ** (-(2 * (d % (HEAD_DIM // 2))).astype(F32) / HEAD_DIM)
    ang = positions.astype(F32)[:, None] * inv_freq[None, :]
    sign = jnp.where(d < HEAD_DIM // 2, -1.0, 1.0).astype(F32)
    return jnp.cos(ang), jnp.sin(ang) * sign[None, :]


def _pair_heads(w, axis):
    shape = w.shape
    split = shape[:axis] + (N_KV_HEADS, Q_GROUP, HEAD_DIM) + shape[axis + 1:]
    return jnp.swapaxes(w.reshape(split), axis, axis + 1).reshape(shape)


def kernel(x, meta_tokens, w_in, w_fourier_out, w_attn_out, w_o, sink_logits, norm_mix_pre, norm_mix_post,
           norm_ffn_pre, norm_ffn_post, w_ffn_gate, w_ffn_up, w_ffn_down):
    batch, seq, _ = x.shape
    depth = w_in.shape[0]
    assert seq % ROW_TILE == 0 and seq // BLOCK >= 3

    h = (x.reshape(batch * seq, D_MODEL),
         jnp.broadcast_to(meta_tokens[None].astype(x.dtype), (batch, N_META, D_MODEL)).reshape(-1, D_MODEL))

    rope_seq = _rope_tables(N_META + jnp.arange(seq))
    rope_meta = _rope_tables(jnp.arange(batch * N_META) % N_META)
    fourier_tables = [jnp.asarray(t).astype(BF16) for t in _fourier_tables(seq)]

    gain = lambda t, l: t[l].reshape(1, D_MODEL)

    for l in range(depth):
        gate0 = IN_WIDTH - GATE_WIDTH
        w_l = jnp.concatenate([_pair_heads(w_in[l, :, :ATTN_WIDTH], 1), w_in[l, :, ATTN_WIDTH:gate0]], axis=1)
        (q, k, v, f), (qm, km, vm, fm) = _inproj(h, gain(norm_mix_pre, l), w_l.astype(BF16),
                                                 rope_seq, rope_meta, seq)
        ya, (w_gate, w_four, w_out) = _attention(sink_logits[l], (q, qm), (k, km), (v, vm), seq, l,
                                                 ((w_in, gate0), (w_fourier_out, 0), (w_o, 0)))
        yf, (wg, wu, wd) = _fourier((f, fm), fourier_tables, seq, l, (w_ffn_gate, w_ffn_up, w_ffn_down))
        h = _mixout(yf, ya, h, gain(norm_mix_pre, l), w_gate, w_four,
                    _pair_heads(w_attn_out[l], 0).astype(BF16), w_out, gain(norm_mix_post, l))
        h = _ffn(h, gain(norm_ffn_pre, l), wg, wu, wd, gain(norm_ffn_post, l))

    return h[0].reshape(batch, seq, D_MODEL)
```
